```python
import math
import jax
import jax.numpy as jnp
from jax import lax
import numpy as np

D_MODEL = 2048
BATCH = 2
SEQ = 4096
DEPTH = 4
DEC_BATCH = 8
DEC_SEQ = 1
PAST_LEN = 16384
PAGE_SIZE = 128

HEAD_DIM = 128
N_HEADS = D_MODEL // HEAD_DIM
MIX_WIDTH = N_HEADS * HEAD_DIM
N_HEADS_A = N_HEADS // 2
N_HEADS_B = N_HEADS - N_HEADS_A
MOBA_BLOCK = 256
MOBA_TOPK = 3
IDX_HEADS = 8
IDX_DIM = 64
DSA_TOPK = 256
ROPE_THETA = 10000.0
RMS_EPS = 1e-6
D_FF = -(-8 * D_MODEL // 768) * 256
N_EVEN_LAYERS = (DEPTH + 1) // 2
N_DSA_LAYERS = DEPTH // 2
EVEN_IN_WIDTH = 3 * MIX_WIDTH
ODD_IN_WIDTH = 3 * MIX_WIDTH + IDX_HEADS * IDX_DIM + IDX_DIM + IDX_HEADS
SB_Q_BLOCK = 128
GATHER_Q_BLOCK = 32

kernel_name = 'moba_stickbreak_dsa_hybrid_step'

F32 = jnp.float32


def rms_norm(x, g):
    xf = x.astype(F32)
    y = xf * lax.rsqrt(jnp.mean(xf * xf, axis=-1, keepdims=True) + RMS_EPS)
    return (y * g.astype(F32)).astype(x.dtype)


def rotary(x, pos):
    d = x.shape[-1]
    half = d // 2
    inv = ROPE_THETA ** (-jnp.arange(half, dtype=F32) / half)
    ang = pos.astype(F32)[:, None] * inv[None, :]
    shp = (1, pos.shape[0]) + (1,) * (x.ndim - 3) + (half,)
    cos = jnp.cos(ang).reshape(shp)
    sin = jnp.sin(ang).reshape(shp)
    xf = x.astype(F32)
    x1, x2 = xf[..., :half], xf[..., half:]
    return jnp.concatenate([x1 * cos - x2 * sin, x2 * cos + x1 * sin], axis=-1).astype(x.dtype)


def map_query_blocks(fn, block, q_pos, *qs):
    lq = q_pos.shape[0]
    qb = math.gcd(lq, block)
    n = lq // qb

    def split(a):
        return jnp.moveaxis(a.reshape((a.shape[0], n, qb) + a.shape[2:]), 1, 0)

    out = lax.map(lambda args: fn(*args), (q_pos.reshape(n, qb),) + tuple(split(a) for a in qs))
    out = jnp.moveaxis(out, 0, 1)
    return out.reshape((out.shape[0], lq) + out.shape[3:])


def moba_attention(q, k, v, q_pos):
    b, lk, h, d = k.shape
    scale = d ** -0.5
    nb = -(-lk // MOBA_BLOCK)
    pad = ((0, 0), (0, nb * MOBA_BLOCK - lk), (0, 0), (0, 0))
    kb = jnp.pad(k, pad).reshape(b, nb, MOBA_BLOCK, h, d).transpose(0, 3, 1, 2, 4)
    vb = jnp.pad(v, pad).reshape(b, nb, MOBA_BLOCK, h, d).transpose(0, 3, 1, 2, 4)
    k_mean = jnp.mean(kb, axis=3, dtype=F32)
    n_sel = min(MOBA_TOPK, nb)
    b_ix = jnp.arange(b)[:, None, None, None]
    h_ix = jnp.arange(h)[None, None, :, None]
    offs = jnp.arange(MOBA_BLOCK)

    def block_fn(pos, qc):
        qn = qc.shape[1]
        own = pos // MOBA_BLOCK
        gate = jnp.einsum('bqhd,bhnd->bqhn', qc, k_mean, preferred_element_type=F32)
        is_past = jnp.arange(nb)[None, :] < own[:, None]
        gate = jnp.where(is_past[None, :, None, :], gate, -jnp.inf)
        top_s, top_i = lax.top_k(gate, n_sel)
        own_b = jnp.broadcast_to(own[None, :, None, None], (b, qn, h, 1)).astype(top_i.dtype)
        blk = jnp.concatenate([top_i, own_b], axis=-1)
        blk_ok = jnp.concatenate([jnp.isfinite(top_s), jnp.ones(own_b.shape, dtype=bool)], axis=-1)
        kg = kb[b_ix, h_ix, blk]
        vg = vb[b_ix, h_ix, blk]
        s = jnp.einsum('bqhd,bqhskd->bqhsk', qc, kg, preferred_element_type=F32) * scale
        key_pos = blk[..., None] * MOBA_BLOCK + offs
        ok = blk_ok[..., None] & (key_pos <= pos[None, :, None, None, None])
        s = jnp.where(ok, s, -jnp.inf).reshape(b, qn, h, -1)
        p = jax.nn.softmax(s, axis=-1)
        o = jnp.einsum('bqhm,bqhmd->bqhd', p, vg.reshape(b, qn, h, -1, d), preferred_element_type=F32)
        return o.astype(qc.dtype)

    return map_query_blocks(block_fn, GATHER_Q_BLOCK, q_pos, q)


def stick_breaking_attention(q, k, v, q_pos):
    lk, d = k.shape[1], k.shape[-1]
    scale = d ** -0.5
    key_pos = jnp.arange(lk)

    def block_fn(pos, qc):
        z = jnp.einsum('bqhd,bkhd->bhqk', qc, k, preferred_element_type=F32) * scale
        before = (key_pos[None, :] < pos[:, None])[None, None]
        log_keep = jnp.where(before, jax.nn.log_sigmoid(-z), 0.0)
        rev = lax.cumsum(log_keep, axis=3, reverse=True)
        after = jnp.concatenate([rev[..., 1:], jnp.zeros_like(rev[..., :1])], axis=-1)
        wts = jnp.where(before, jnp.exp(jax.nn.log_sigmoid(z) + after), 0.0)
        o = jnp.einsum('bhqk,bkhd->bqhd', wts, v, preferred_element_type=F32)
        return o.astype(qc.dtype)

    return map_query_blocks(block_fn, SB_Q_BLOCK, q_pos, q)


def dsa_attention(q, k, v, q_idx, k_idx, w_idx, q_pos):
    b, lk, h, d = k.shape
    scale = d ** -0.5
    n_keep = min(DSA_TOPK, max(lk // 4, 1))
    key_pos = jnp.arange(lk)
    b_ix = jnp.arange(b)[:, None, None]

    def block_fn(pos, qc, qic, wc):
        rel = jax.nn.relu(jnp.einsum('bqge,bke->bqgk', qic, k_idx, preferred_element_type=F32))
        score = jnp.einsum('bqg,bqgk->bqk', wc.astype(F32), rel)
        score = jnp.where((key_pos[None, :] <= pos[:, None])[None], score, -jnp.inf)
        top_s, top_i = lax.top_k(score, n_keep)
        kg = k[b_ix, top_i]
        vg = v[b_ix, top_i]
        s = jnp.einsum('bqhd,bqnhd->bqhn', qc, kg, preferred_element_type=F32) * scale
        s = jnp.where(jnp.isfinite(top_s)[:, :, None, :], s, -jnp.inf)
        p = jax.nn.softmax(s, axis=-1)
        o = jnp.einsum('bqhn,bqnhd->bqhd', p, vg, preferred_element_type=F32)
        return o.astype(qc.dtype)

    return map_query_blocks(block_fn, GATHER_Q_BLOCK, q_pos, q, q_idx, w_idx)


def with_past(past, new):
    return new if past is None else jnp.concatenate([past, new], axis=1)


def even_mixer(h, q_pos, w_in, w_out, past_k, past_v):
    b, l, _ = h.shape
    proj = jnp.einsum('bld,de->ble', h, w_in).reshape(b, l, 3, N_HEADS, HEAD_DIM)
    q, k, v = proj[:, :, 0], proj[:, :, 1], proj[:, :, 2]
    q = jnp.concatenate([rotary(q[:, :, :N_HEADS_A], q_pos), q[:, :, N_HEADS_A:]], axis=2)
    k = jnp.concatenate([rotary(k[:, :, :N_HEADS_A], q_pos), k[:, :, N_HEADS_A:]], axis=2)
    k_all = with_past(past_k, k)
    v_all = with_past(past_v, v)
    oa = moba_attention(q[:, :, :N_HEADS_A], k_all[:, :, :N_HEADS_A], v_all[:, :, :N_HEADS_A], q_pos)
    ob = stick_breaking_attention(q[:, :, N_HEADS_A:], k_all[:, :, N_HEADS_A:], v_all[:, :, N_HEADS_A:], q_pos)
    o = jnp.concatenate([oa, ob], axis=2).reshape(b, l, MIX_WIDTH)
    return jnp.einsum('ble,ed->bld', o, w_out), k, v


def odd_mixer(h, q_pos, w_in, w_out, past_k, past_v, past_kidx):
    b, l, _ = h.shape
    proj = jnp.einsum('bld,de->ble', h, w_in)
    c0 = 3 * MIX_WIDTH
    c1 = c0 + IDX_HEADS * IDX_DIM
    c2 = c1 + IDX_DIM
    qkv, qi, ki, wi = jnp.split(proj, [c0, c1, c2], axis=-1)
    qkv = qkv.reshape(b, l, 3, N_HEADS, HEAD_DIM)
    q = rotary(qkv[:, :, 0], q_pos)
    k = rotary(qkv[:, :, 1], q_pos)
    v = qkv[:, :, 2]
    qi = rotary(qi.reshape(b, l, IDX_HEADS, IDX_DIM), q_pos)
    ki = rotary(ki, q_pos)
    o = dsa_attention(q, with_past(past_k, k), with_past(past_v, v), qi, with_past(past_kidx, ki), wi, q_pos)
    return jnp.einsum('ble,ed->bld', o.reshape(b, l, MIX_WIDTH), w_out), k, v, ki


def swiglu(h, wg, wu, wd):
    a = jnp.einsum('bld,df->blf', h, wg)
    u = jnp.einsum('bld,df->blf', h, wu)
    return jnp.einsum('blf,fd->bld', jax.nn.silu(a) * u, wd)


def trunk(x, q_pos, read_past, attn_norm, ffn_norm, final_norm, w_in_even, w_out_even,
          w_in_odd, w_out_odd, w_gate, w_up, w_down):
    new_k, new_v, new_kidx = [], [], []
    for i in range(DEPTH):
        h = rms_norm(x, attn_norm[i])
        if i % 2 == 0:
            o, k, v = even_mixer(h, q_pos, w_in_even[i // 2], w_out_even[i // 2], *read_past(i))
        else:
            o, k, v, ki = odd_mixer(h, q_pos, w_in_odd[i // 2], w_out_odd[i // 2], *read_past(i))
            new_kidx.append(ki)
        new_k.append(k)
        new_v.append(v)
        x = x + o
        x = x + swiglu(rms_norm(x, ffn_norm[i]), w_gate[i], w_up[i], w_down[i])
    return rms_norm(x, final_norm), jnp.stack(new_k), jnp.stack(new_v), jnp.stack(new_kidx)


def setup_inputs(seed: int = 0) -> dict:
    key = jax.random.key(seed)
    ks = jax.random.split(key, 18)
    n_pages = PAST_LEN // PAGE_SIZE
    n_used = DEC_BATCH * n_pages
    pool = n_used + max(1, n_used // 4)
    page_table = jax.random.permutation(ks[0], pool)[:n_used].reshape(DEC_BATCH, n_pages).astype(jnp.int32)

    def nrm(k, shape, scale):
        return jax.random.normal(k, shape, F32) * scale

    return {
        'x_prompt': nrm(ks[1], (BATCH, SEQ, D_MODEL), 1.0),
        'x_sample': nrm(ks[2], (DEC_BATCH, DEC_SEQ, D_MODEL), 1.0),
        'cache_k': nrm(ks[3], (DEPTH, pool, PAGE_SIZE, N_HEADS, HEAD_DIM), 1.0),
        'cache_v': nrm(ks[4], (DEPTH, pool, PAGE_SIZE, N_HEADS, HEAD_DIM), 1.0),
        'cache_kidx': nrm(ks[5], (N_DSA_LAYERS, pool, PAGE_SIZE, IDX_DIM), 1.0),
        'page_table': page_table,
        'attn_norm': 1.0 + nrm(ks[6], (DEPTH, D_MODEL), 0.02),
        'ffn_norm': 1.0 + nrm(ks[7], (DEPTH, D_MODEL), 0.02),
        'final_norm': 1.0 + nrm(ks[8], (D_MODEL,), 0.02),
        'w_in_even': nrm(ks[9], (N_EVEN_LAYERS, D_MODEL, EVEN_IN_WIDTH), D_MODEL ** -0.5),
        'w_out_even': nrm(ks[10], (N_EVEN_LAYERS, MIX_WIDTH, D_MODEL), MIX_WIDTH ** -0.5),
        'w_in_odd': nrm(ks[11], (N_DSA_LAYERS, D_MODEL, ODD_IN_WIDTH), D_MODEL ** -0.5),
        'w_out_odd': nrm(ks[12], (N_DSA_LAYERS, MIX_WIDTH, D_MODEL), MIX_WIDTH ** -0.5),
        'w_gate': nrm(ks[13], (DEPTH, D_MODEL, D_FF), D_MODEL ** -0.5),
        'w_up': nrm(ks[14], (DEPTH, D_MODEL, D_FF), D_MODEL ** -0.5),
        'w_down': nrm(ks[15], (DEPTH, D_FF, D_MODEL), D_FF ** -0.5),
    }


def reference(x_prompt, x_sample, cache_k, cache_v, cache_kidx, page_table, attn_norm, ffn_norm,
              final_norm, w_in_even, w_out_even, w_in_odd, w_out_odd, w_gate, w_up, w_down):
    weights = (attn_norm, ffn_norm, final_norm, w_in_even, w_out_even, w_in_odd, w_out_odd,
               w_gate, w_up, w_down)
    db = page_table.shape[0]
    past_len = page_table.shape[1] * cache_k.shape[2]

    def no_past(i):
        return (None, None) if i % 2 == 0 else (None, None, None)

    def paged_past(i):
        pk = cache_k[i, page_table].reshape((db, past_len) + cache_k.shape[3:])
        pv = cache_v[i, page_table].reshape((db, past_len) + cache_v.shape[3:])
        if i % 2 == 0:
            return (pk, pv)
        pki = cache_kidx[i // 2, page_table].reshape((db, past_len) + cache_kidx.shape[3:])
        return (pk, pv, pki)

    pos_prompt = jnp.arange(x_prompt.shape[1], dtype=jnp.int32)
    pos_sample = past_len + jnp.arange(x_sample.shape[1], dtype=jnp.int32)
    y_prompt, new_k_prompt, new_v_prompt, new_kidx_prompt = trunk(x_prompt, pos_prompt, no_past, *weights)
    y_sample, new_k_sample, new_v_sample, new_kidx_sample = trunk(x_sample, pos_sample, paged_past, *weights)
    return (y_prompt, y_sample, new_k_prompt, new_v_prompt, new_kidx_prompt, new_k_sample, new_v_sample, new_kidx_sample)
```

```python
import functools

import jax
import jax.numpy as jnp
from jax import lax
from jax.experimental import pallas as pl
from jax.experimental.pallas import tpu as pltpu

F32 = jnp.float32
BF16 = jnp.bfloat16
NEG_INF = float("-inf")

HEAD_DIM = 128
N_HEADS = 16
N_HEADS_MOBA = 8
MOBA_BLOCK = 256
MOBA_TOPK = 3
IDX_HEADS = 8
IDX_DIM = 64
DSA_TOPK = 256
ROPE_THETA = 10000.0
RMS_EPS = 1e-6
PAGE_SIZE = 128

LANES = 128
VMEM_LIMIT = 56 * 1024 * 1024

NT_DIMS = (((1,), (1,)), ((), ()))


def _cparams(*sem):
    return pltpu.CompilerParams(dimension_semantics=sem, vmem_limit_bytes=VMEM_LIMIT)


def _dot(a, b):
    return jnp.dot(a, b, preferred_element_type=F32)


def _dot_nt(a, b):
    return lax.dot_general(a, b, NT_DIMS, preferred_element_type=F32)


def _split3(x):
    hi = x.astype(BF16)
    r1 = x - hi.astype(F32)
    mid = r1.astype(BF16)
    lo = (r1 - mid.astype(F32)).astype(BF16)
    return hi, mid, lo


def _split2(x):
    hi = x.astype(BF16)
    return hi, (x - hi.astype(F32)).astype(BF16)


def _rope_cols(y, cos_ref, sin_ref, half):
    tn = y.shape[1]
    wide = cos_ref.shape[1] == tn and tn != LANES
    outs = []
    lane = lax.broadcasted_iota(jnp.int32, (y.shape[0], LANES), 1)
    for s in range(tn // LANES):
        ys = y[:, s * LANES:(s + 1) * LANES]
        if wide:
            c = cos_ref[:, s * LANES:(s + 1) * LANES]
            sn = sin_ref[:, s * LANES:(s + 1) * LANES]
        else:
            c = cos_ref[...]
            sn = sin_ref[...]
        if 2 * half == LANES:
            partner = pltpu.roll(ys, half, 1)
        else:
            partner = jnp.where(lane % (2 * half) < half,
                                pltpu.roll(ys, LANES - half, 1), pltpu.roll(ys, half, 1))
        outs.append(ys * c + partner * sn)
    return jnp.concatenate(outs, axis=1) if len(outs) > 1 else outs[0]


def _norm_proj_kernel(x_ref, g_ref, w_ref, cos_ref, sin_ref, *refs, n_col_tiles, rope_tiles, half,
                      want_f32, want_bf16, want_kmean):
    refs = list(refs)
    hn_ref = refs.pop()
    o32_ref = refs.pop(0) if want_f32 else None
    o16_ref = refs.pop(0) if want_bf16 else None
    km_ref = refs.pop(0) if want_kmean else None
    j = pl.program_id(1)

    @pl.when(j == 0)
    def _():
        x = x_ref[...]
        ms = jnp.mean(x * x, axis=-1, keepdims=True)
        hn_ref[...] = (x * lax.rsqrt(ms + RMS_EPS) * g_ref[...]).astype(BF16)

    acc = _dot(hn_ref[...], w_ref[...])

    def emit(y):
        if want_f32:
            o32_ref[...] = y
        if want_bf16:
            o16_ref[...] = y.astype(BF16)
        if want_kmean:
            for r in range(y.shape[0] // MOBA_BLOCK):
                km_ref[r] = jnp.mean(y[r * MOBA_BLOCK:(r + 1) * MOBA_BLOCK], axis=0, keepdims=True)

    if rope_tiles == 0:
        emit(acc)
    elif rope_tiles >= n_col_tiles:
        emit(_rope_cols(acc, cos_ref, sin_ref, half))
    else:
        @pl.when(j < rope_tiles)
        def _():
            emit(_rope_cols(acc, cos_ref, sin_ref, half))

        @pl.when(j >= rope_tiles)
        def _():
            emit(acc)


def norm_proj(x, g, w, cos, sin, *, col0, ncols, tm, tn, rope_tiles, half,
              want_f32=False, want_bf16=False, want_kmean=False):
    t, d = x.shape
    assert t % tm == 0 and ncols % tn == 0 and col0 % tn == 0 and cos.shape[0] % tm == 0
    pos_blocks = cos.shape[0] // tm
    cb0 = col0 // tn
    n_col_tiles = ncols // tn
    out_shape, out_specs = [], []
    if want_f32:
        out_shape.append(jax.ShapeDtypeStruct((t, ncols), F32))
        out_specs.append(pl.BlockSpec((tm, tn), lambda i, j: (i, j)))
    if want_bf16:
        out_shape.append(jax.ShapeDtypeStruct((t, ncols), BF16))
        out_specs.append(pl.BlockSpec((tm, tn), lambda i, j: (i, j)))
    if want_kmean:
        assert tm % MOBA_BLOCK == 0
        out_shape.append(jax.ShapeDtypeStruct((t // MOBA_BLOCK, 1, ncols), F32))
        out_specs.append(pl.BlockSpec((tm // MOBA_BLOCK, 1, tn), lambda i, j: (i, 0, j)))
    kern = functools.partial(_norm_proj_kernel, n_col_tiles=n_col_tiles, rope_tiles=rope_tiles,
                             half=half, want_f32=want_f32, want_bf16=want_bf16, want_kmean=want_kmean)
    tw = cos.shape[1]
    return pl.pallas_call(
        kern,
        grid=(t // tm, n_col_tiles),
        in_specs=[
            pl.BlockSpec((tm, d), lambda i, j: (i, 0)),
            pl.BlockSpec((1, d), lambda i, j: (0, 0)),
            pl.BlockSpec((d, tn), lambda i, j: (0, cb0 + j)),
            pl.BlockSpec((tm, tw), lambda i, j: (i % pos_blocks, 0)),
            pl.BlockSpec((tm, tw), lambda i, j: (i % pos_blocks, 0)),
        ],
        out_specs=out_specs,
        out_shape=out_shape,
        scratch_shapes=[pltpu.VMEM((tm, d), BF16)],
        compiler_params=_cparams("parallel", "arbitrary"),
        name="norm_proj",
    )(x, g, w, cos, sin)


def _matmul_res_kernel(a_ref, w_ref, r_ref, o_ref):
    o_ref[...] = r_ref[...] + _dot(a_ref[...], w_ref[...])


def matmul_res(a, w, res, *, tm, tn):
    t, k = a.shape
    n = w.shape[1]
    assert t % tm == 0 and n % tn == 0
    return pl.pallas_call(
        _matmul_res_kernel,
        grid=(t // tm, n // tn),
        in_specs=[
            pl.BlockSpec((tm, k), lambda i, j: (i, 0)),
            pl.BlockSpec((k, tn), lambda i, j: (0, j)),
            pl.BlockSpec((tm, tn), lambda i, j: (i, j)),
        ],
        out_specs=pl.BlockSpec((tm, tn), lambda i, j: (i, j)),
        out_shape=jax.ShapeDtypeStruct((t, n), F32),
        compiler_params=_cparams("parallel", "arbitrary"),
        name="matmul_res",
    )(a, w, res)


def _gate_up_kernel(x_ref, g_ref, wg_ref, wu_ref, o_ref, hn_ref):
    @pl.when(pl.program_id(1) == 0)
    def _():
        x = x_ref[...]
        ms = jnp.mean(x * x, axis=-1, keepdims=True)
        hn_ref[...] = (x * lax.rsqrt(ms + RMS_EPS) * g_ref[...]).astype(BF16)

    hn = hn_ref[...]
    a = _dot(hn, wg_ref[...])
    u = _dot(hn, wu_ref[...])
    o_ref[...] = (a / (1.0 + jnp.exp(-a)) * u).astype(BF16)


def gate_up(x, g, wg, wu, *, tm, tn):
    t, d = x.shape
    f = wg.shape[1]
    assert t % tm == 0 and f % tn == 0
    return pl.pallas_call(
        _gate_up_kernel,
        grid=(t // tm, f // tn),
        in_specs=[
            pl.BlockSpec((tm, d), lambda i, j: (i, 0)),
            pl.BlockSpec((1, d), lambda i, j: (0, 0)),
            pl.BlockSpec((d, tn), lambda i, j: (0, j)),
            pl.BlockSpec((d, tn), lambda i, j: (0, j)),
        ],
        out_specs=pl.BlockSpec((tm, tn), lambda i, j: (i, j)),
        out_shape=jax.ShapeDtypeStruct((t, f), BF16),
        scratch_shapes=[pltpu.VMEM((tm, d), BF16)],
        compiler_params=_cparams("parallel", "arbitrary"),
        name="gate_up",
    )(x, g, wg, wu)


def _rmsnorm_kernel(x_ref, g_ref, o_ref):
    x = x_ref[...]
    ms = jnp.mean(x * x, axis=-1, keepdims=True)
    o_ref[...] = x * lax.rsqrt(ms + RMS_EPS) * g_ref[...]


def rmsnorm(x, g, *, tm):
    t, d = x.shape
    assert t % tm == 0
    return pl.pallas_call(
        _rmsnorm_kernel,
        grid=(t // tm,),
        in_specs=[pl.BlockSpec((tm, d), lambda i: (i, 0)), pl.BlockSpec((1, d), lambda i: (0, 0))],
        out_specs=pl.BlockSpec((tm, d), lambda i: (i, 0)),
        out_shape=jax.ShapeDtypeStruct((t, d), F32),
        compiler_params=_cparams("parallel"),
        name="rmsnorm",
    )(x, g)


def _online_softmax_step(s, v, m, l, acc):
    m_new = jnp.maximum(m, jnp.max(s, axis=-1, keepdims=True))
    m_safe = jnp.where(m_new == NEG_INF, 0.0, m_new)
    alpha = jnp.exp(m - m_safe)
    p = jnp.exp(s - m_safe)
    l = alpha * l + jnp.sum(p, axis=-1, keepdims=True)
    acc = alpha * acc + _dot(p.astype(BF16), v)
    return m_new, l, acc


def _moba_tile(q_ref, k_ref, v_ref, km_ref, o_ref, qi, scale):
    tq = q_ref.shape[1]
    blk = MOBA_BLOCK
    nb = km_ref.shape[1]
    q = q_ref[0]
    km_hi, km_mid, km_lo = _split3(km_ref[0])
    gate = _dot_nt(q, km_hi) + _dot_nt(q, km_mid) + _dot_nt(q, km_lo)
    n_idx = lax.broadcasted_iota(jnp.int32, (tq, nb), 1)
    g = jnp.where(n_idx < qi, gate, NEG_INF)
    sel = jnp.zeros((tq, nb), dtype=jnp.bool_)
    for _ in range(min(MOBA_TOPK, nb)):
        mx = jnp.max(g, axis=-1, keepdims=True)
        first = jnp.min(jnp.where(g == mx, n_idx, nb), axis=-1, keepdims=True)
        hit = n_idx == first
        sel = sel | (hit & (mx > NEG_INF))
        g = jnp.where(hit, NEG_INF, g)
    bias = jnp.where(sel, 0.0, NEG_INF)

    row = lax.broadcasted_iota(jnp.int32, (tq, blk), 0)
    col = lax.broadcasted_iota(jnp.int32, (tq, blk), 1)
    d0 = pl.multiple_of(qi * blk, blk)
    s = jnp.where(col <= row, _dot_nt(q, k_ref[0, pl.ds(d0, blk), :]) * scale, NEG_INF)
    m0 = jnp.full((tq, 1), NEG_INF, F32)
    l0 = jnp.zeros((tq, 1), F32)
    a0 = jnp.zeros((tq, HEAD_DIM), F32)
    carry = _online_softmax_step(s, v_ref[0, pl.ds(d0, blk), :], m0, l0, a0)

    def body(j, carry):
        k0 = pl.multiple_of(j * blk, blk)
        bcol = jnp.max(jnp.where(n_idx == j, bias, NEG_INF), axis=-1, keepdims=True)
        s = _dot_nt(q, k_ref[0, pl.ds(k0, blk), :]) * scale + bcol
        return _online_softmax_step(s, v_ref[0, pl.ds(k0, blk), :], *carry)

    m, l, acc = lax.fori_loop(0, qi, body, carry)
    o_ref[0] = (acc / l).astype(o_ref.dtype)


def _sb_block(q, k, v, tri, carry, acc, scale, before):
    z = _dot_nt(q, k) * scale
    ls = jnp.minimum(z, 0.0) - jnp.log1p(jnp.exp(-jnp.abs(z)))
    lk = ls - z
    if before is not None:
        lk = jnp.where(before, lk, 0.0)
    lk_hi, lk_lo = _split2(lk)
    cs = _dot(lk_hi, tri) + _dot(lk_lo, tri)
    w = jnp.exp(ls + carry + cs)
    if before is not None:
        w = jnp.where(before, w, 0.0)
    acc = acc + _dot(w.astype(BF16), v)
    carry = carry + cs[:, 0:1] + lk[:, 0:1]
    return carry, acc


def _sb_tile(q_ref, k_ref, v_ref, o_ref, qi, scale):
    tq = q_ref.shape[1]
    blk = tq
    q = q_ref[0]
    row = lax.broadcasted_iota(jnp.int32, (tq, blk), 0)
    col = lax.broadcasted_iota(jnp.int32, (tq, blk), 1)
    tri = jnp.where(row > col, 1.0, 0.0).astype(BF16)
    d0 = pl.multiple_of(qi * blk, blk)
    carry = jnp.zeros((tq, 1), F32)
    acc = jnp.zeros((tq, HEAD_DIM), F32)
    carry, acc = _sb_block(q, k_ref[0, pl.ds(d0, blk), :], v_ref[0, pl.ds(d0, blk), :], tri,
                           carry, acc, scale, col < row)

    def body(jj, c):
        k0 = pl.multiple_of((qi - 1 - jj) * blk, blk)
        return _sb_block(q, k_ref[0, pl.ds(k0, blk), :], v_ref[0, pl.ds(k0, blk), :], tri,
                         c[0], c[1], scale, None)

    carry, acc = lax.fori_loop(0, qi, body, (carry, acc))
    o_ref[0] = acc.astype(o_ref.dtype)


def _even_attn_kernel(q_ref, k_ref, v_ref, km_ref, o_ref, *, scale):
    h = pl.program_id(1)
    qi = pl.program_id(2)

    @pl.when(h < N_HEADS_MOBA)
    def _():
        _moba_tile(q_ref, k_ref, v_ref, km_ref, o_ref, qi, scale)

    @pl.when(h >= N_HEADS_MOBA)
    def _():
        _sb_tile(q_ref, k_ref, v_ref, o_ref, qi, scale)


def even_attention(q, k, v, kmean):
    b, l, w = q.shape
    nh = w // HEAD_DIM
    tq = MOBA_BLOCK
    assert l % tq == 0 and nh == N_HEADS
    nb = l // MOBA_BLOCK
    return pl.pallas_call(
        functools.partial(_even_attn_kernel, scale=HEAD_DIM ** -0.5),
        grid=(b, nh, l // tq),
        in_specs=[
            pl.BlockSpec((1, tq, HEAD_DIM), lambda bi, h, i: (bi, i, h)),
            pl.BlockSpec((1, l, HEAD_DIM), lambda bi, h, i: (bi, 0, h)),
            pl.BlockSpec((1, l, HEAD_DIM), lambda bi, h, i: (bi, 0, h)),
            pl.BlockSpec((1, nb, HEAD_DIM), lambda bi, h, i: (bi, 0, h)),
        ],
        out_specs=pl.BlockSpec((1, tq, HEAD_DIM), lambda bi, h, i: (bi, i, h)),
        out_shape=jax.ShapeDtypeStruct((b, l, w), BF16),
        compiler_params=_cparams("parallel", "parallel", "arbitrary"),
        name="even_attention",
    )(q, k, v, kmean)


def _key_to_f32(u):
    t = u ^ jnp.int32(-2 ** 31)
    bits = t ^ ((t >> 31) & jnp.int32(0x7FFFFFFF))
    return lax.bitcast_convert_type(bits, F32)


def _dsa_build_bias(qi_ref, ki_ref, wi_ref, s_ref, t, n_keep):
    tq = s_ref.shape[0]
    ck = tq
    n_chunks = t + 1
    row = lax.broadcasted_iota(jnp.int32, (tq, ck), 0)
    col = lax.broadcasted_iota(jnp.int32, (tq, ck), 1)
    wi = wi_ref[0]

    def score_chunk(c, _):
        k0 = pl.multiple_of(c * ck, ck)
        kc = ki_ref[0, pl.ds(k0, ck), :]
        sc = jnp.zeros((tq, ck), F32)
        for g in range(IDX_HEADS):
            sc = sc + wi[:, g:g + 1] * jnp.maximum(_dot_nt(qi_ref[0, g], kc), 0.0)
        s_ref[:, pl.ds(k0, ck)] = jnp.where((c < t) | (col <= row), sc, NEG_INF)
        return 0

    lax.fori_loop(0, n_chunks, score_chunk, 0)

    def count(pred):
        def body(c, acc):
            k0 = pl.multiple_of(c * ck, ck)
            return acc + jnp.where(pred(s_ref[:, pl.ds(k0, ck)], c), 1.0, 0.0)
        acc = lax.fori_loop(0, n_chunks, body, jnp.zeros((tq, ck), F32))
        return jnp.sum(acc, axis=-1, keepdims=True)

    def bit_step(i, u):
        cand = u | jnp.left_shift(jnp.int32(1), 31 - i)
        thr = _key_to_f32(cand)
        return jnp.where(count(lambda s, c: s >= thr) >= n_keep, cand, u)

    u = lax.fori_loop(0, 32, bit_step, jnp.zeros((tq, 1), jnp.int32))
    thr = _key_to_f32(u)
    pos = t * tq + lax.broadcasted_iota(jnp.int32, (tq, 1), 0)
    keep_all = pos < n_keep
    n_ge = count(lambda s, c: s >= thr)
    tied = (n_ge > n_keep) & jnp.logical_not(keep_all)
    cut_ref_val = jnp.full((tq, 1), 2 ** 30, jnp.int32)

    def tie_break():
        need = n_keep - count(lambda s, c: s > thr)
        n_bits = max(1, int((s_ref.shape[1] - 1)).bit_length())

        def idx_step(i, x):
            cand = x | jnp.left_shift(jnp.int32(1), n_bits - 1 - i)
            below = count(lambda s, c: (s == thr) & (c * ck + col < cand))
            return jnp.where(below < need, cand, x)

        x = lax.fori_loop(0, n_bits, idx_step, jnp.zeros((tq, 1), jnp.int32))
        return jnp.where(tied, x, cut_ref_val)

    any_tied = jnp.max(jnp.where(tied, 1, 0)) > 0
    cut = lax.cond(any_tied, tie_break, lambda: cut_ref_val)

    def to_bias(c, _):
        k0 = pl.multiple_of(c * ck, ck)
        s = s_ref[:, pl.ds(k0, ck)]
        kept = (s > thr) | ((s == thr) & (c * ck + col <= cut)) | keep_all
        s_ref[:, pl.ds(k0, ck)] = jnp.where(kept & (s > NEG_INF), 0.0, NEG_INF)
        return 0

    lax.fori_loop(0, n_chunks, to_bias, 0)


def _dsa_attn_kernel(q_ref, k_ref, v_ref, qi_ref, ki_ref, wi_ref, o_ref, s_ref, *, scale, n_keep):
    t = pl.program_id(1)
    h = pl.program_id(2)
    tq = q_ref.shape[1]
    ck = tq

    @pl.when(h == 0)
    def _():
        _dsa_build_bias(qi_ref, ki_ref, wi_ref, s_ref, t, n_keep)

    q = q_ref[0]

    def body(c, carry):
        k0 = pl.multiple_of(c * ck, ck)
        s = _dot_nt(q, k_ref[0, pl.ds(k0, ck), :]) * scale + s_ref[:, pl.ds(k0, ck)]
        return _online_softmax_step(s, v_ref[0, pl.ds(k0, ck), :], *carry)

    init = (jnp.full((tq, 1), NEG_INF, F32), jnp.zeros((tq, 1), F32), jnp.zeros((tq, HEAD_DIM), F32))
    m, l, acc = lax.fori_loop(0, t + 1, body, init)
    o_ref[0] = (acc / l).astype(o_ref.dtype)


def dsa_attention(q, k, v, qidx, kidx, widx):
    b, l, w = q.shape
    nh = w // HEAD_DIM
    tq = 256
    assert l % tq == 0
    n_keep = min(DSA_TOPK, max(l // 4, 1))
    g = qidx.shape[1]
    return pl.pallas_call(
        functools.partial(_dsa_attn_kernel, scale=HEAD_DIM ** -0.5, n_keep=n_keep),
        grid=(b, l // tq, nh),
        in_specs=[
            pl.BlockSpec((1, tq, HEAD_DIM), lambda bi, i, h: (bi, i, h)),
            pl.BlockSpec((1, l, HEAD_DIM), lambda bi, i, h: (bi, 0, h)),
            pl.BlockSpec((1, l, HEAD_DIM), lambda bi, i, h: (bi, 0, h)),
            pl.BlockSpec((1, g, tq, IDX_DIM), lambda bi, i, h: (bi, 0, i, 0)),
            pl.BlockSpec((1, l, IDX_DIM), lambda bi, i, h: (bi, 0, 0)),
            pl.BlockSpec((1, tq, g), lambda bi, i, h: (bi, i, 0)),
        ],
        out_specs=pl.BlockSpec((1, tq, HEAD_DIM), lambda bi, i, h: (bi, i, h)),
        out_shape=jax.ShapeDtypeStruct((b, l, w), BF16),
        scratch_shapes=[pltpu.VMEM((tq, l), F32)],
        compiler_params=_cparams("parallel", "arbitrary", "arbitrary"),
        name="dsa_attention",
    )(q, k, v, qidx, kidx, widx)


def _block_diag_rows(qrow, n_heads):
    w = n_heads * HEAD_DIM
    r = lax.broadcasted_iota(jnp.int32, (n_heads, w), 0)
    c = lax.broadcasted_iota(jnp.int32, (n_heads, w), 1)
    return jnp.where(c // HEAD_DIM == r, jnp.broadcast_to(qrow, (n_heads, w)), 0.0)


def _diag_blocks(a, n_heads):
    return jnp.concatenate([a[h:h + 1, h * HEAD_DIM:(h + 1) * HEAD_DIM] for h in range(n_heads)], axis=1)


def _even_decode_kernel(pt_ref, q_ref, *refs, pg, n_groups, scale):
    k_refs = refs[:pg]
    v_refs = refs[pg:2 * pg]
    ids_ref, osb_ref, km_ref, qsb_ref, carry_ref, acc_ref = refs[2 * pg:]
    p = pl.program_id(1)
    group = n_groups - 1 - p
    wa = N_HEADS_MOBA * HEAD_DIM
    nsb = N_HEADS - N_HEADS_MOBA
    pages_per_block = MOBA_BLOCK // PAGE_SIZE

    @pl.when(p == 0)
    def _():
        qsb_ref[...] = _block_diag_rows(q_ref[0][:, wa:], nsb).astype(BF16)
        carry_ref[...] = jnp.zeros_like(carry_ref)
        acc_ref[...] = jnp.zeros_like(acc_ref)

    row = lax.broadcasted_iota(jnp.int32, (PAGE_SIZE, PAGE_SIZE), 0)
    col = lax.broadcasted_iota(jnp.int32, (PAGE_SIZE, PAGE_SIZE), 1)
    tri = jnp.where(row > col, 1.0, 0.0).astype(BF16)
    qsb = qsb_ref[...]
    carry = carry_ref[...]
    acc = acc_ref[...]
    ksum = None
    for u in reversed(range(pg)):
        kp = k_refs[u][0, 0]
        part = jnp.sum(kp[:, :wa], axis=0, keepdims=True)
        ksum = part if ksum is None else ksum + part
        if u % pages_per_block == 0:
            blk = group * (pg // pages_per_block) + u // pages_per_block
            km_ref[pl.ds(blk, 1), :] = ksum * (1.0 / MOBA_BLOCK)
            ksum = None
        carry, acc = _sb_block(qsb, kp[:, wa:].astype(BF16), v_refs[u][0, 0].astype(BF16), tri,
                               carry, acc, scale, None)
    carry_ref[...] = carry
    acc_ref[...] = acc

    @pl.when(p == n_groups - 1)
    def _():
        osb_ref[0] = _diag_blocks(acc, nsb)
        nb = km_ref.shape[0]
        qa = _split3(_block_diag_rows(q_ref[0][:, :wa], N_HEADS_MOBA))
        ka = _split3(km_ref[...])
        gate = jnp.zeros((N_HEADS_MOBA, nb), F32)
        for i, j in ((2, 0), (0, 2), (1, 1), (1, 0), (0, 1), (0, 0)):
            gate = gate + _dot_nt(qa[i], ka[j])
        n_idx = lax.broadcasted_iota(jnp.int32, gate.shape, 1)
        lane = lax.broadcasted_iota(jnp.int32, (N_HEADS_MOBA, LANES), 1)
        ids = jnp.zeros((N_HEADS_MOBA, LANES), jnp.int32)
        for r in range(MOBA_TOPK):
            mx = jnp.max(gate, axis=-1, keepdims=True)
            first = jnp.min(jnp.where(gate == mx, n_idx, nb), axis=-1, keepdims=True)
            ids = jnp.where(lane == r, first, ids)
            gate = jnp.where(n_idx == first, NEG_INF, gate)
        ids_ref[0] = ids


def even_decode_sweep(page_table, q, cache_k, cache_v, layer, pg=4):
    s, n_pages = page_table.shape
    assert n_pages % pg == 0 and pg % (MOBA_BLOCK // PAGE_SIZE) == 0
    n_groups = n_pages // pg
    nb = n_pages * PAGE_SIZE // MOBA_BLOCK
    assert nb >= MOBA_TOPK
    w = N_HEADS * HEAD_DIM
    wa = N_HEADS_MOBA * HEAD_DIM

    def kmap(u):
        return lambda b, p, pt: (layer, pt[b, (n_groups - 1 - p) * pg + u], 0, 0)

    def vmap_(u):
        return lambda b, p, pt: (layer, pt[b, (n_groups - 1 - p) * pg + u], 0, 1)

    grid_spec = pltpu.PrefetchScalarGridSpec(
        num_scalar_prefetch=1,
        grid=(s, n_groups),
        in_specs=[pl.BlockSpec((1, 1, w), lambda b, p, pt: (b, 0, 0))]
        + [pl.BlockSpec((1, 1, PAGE_SIZE, w), kmap(u)) for u in range(pg)]
        + [pl.BlockSpec((1, 1, PAGE_SIZE, w - wa), vmap_(u)) for u in range(pg)],
        out_specs=[pl.BlockSpec((1, N_HEADS_MOBA, LANES), lambda b, p, pt: (b, 0, 0)),
                   pl.BlockSpec((1, 1, w - wa), lambda b, p, pt: (b, 0, 0))],
        scratch_shapes=[pltpu.VMEM((nb, wa), F32), pltpu.VMEM((N_HEADS - N_HEADS_MOBA, w - wa), BF16),
                        pltpu.VMEM((N_HEADS - N_HEADS_MOBA, 1), F32),
                        pltpu.VMEM((N_HEADS - N_HEADS_MOBA, w - wa), F32)],
    )
    return pl.pallas_call(
        functools.partial(_even_decode_kernel, pg=pg, n_groups=n_groups, scale=HEAD_DIM ** -0.5),
        grid_spec=grid_spec,
        out_shape=[jax.ShapeDtypeStruct((s, N_HEADS_MOBA, LANES), jnp.int32),
                   jax.ShapeDtypeStruct((s, 1, w - wa), F32)],
        compiler_params=_cparams("parallel", "arbitrary"),
        name="even_decode_sweep",
    )(page_table, q, *([cache_k] * pg), *([cache_v] * pg))


def _moba_decode_kernel(pt_ref, ids_ref, q_ref, kn_ref, vn_ref, *refs, n_pages_sel, scale):
    k_refs = refs[:n_pages_sel]
    v_refs = refs[n_pages_sel:2 * n_pages_sel]
    o_ref = refs[2 * n_pages_sel]
    q = q_ref[0]
    q8 = jnp.broadcast_to(q, (8, HEAD_DIM)).astype(BF16)
    qr = q.astype(BF16).astype(F32)
    knew = kn_ref[0].astype(BF16).astype(F32)
    s_own = jnp.sum(qr * knew, axis=-1, keepdims=True) * scale
    scores = [_dot_nt(q8, k_refs[u][0, 0].astype(BF16)) * scale for u in range(n_pages_sel)]
    m = s_own
    for sc in scores:
        m = jnp.maximum(m, jnp.max(sc[0:1], axis=-1, keepdims=True))
    p_own = jnp.exp(s_own - m)
    l = p_own
    acc = p_own * vn_ref[0].astype(BF16).astype(F32)
    for u, sc in enumerate(scores):
        pu = jnp.exp(sc - m)
        l = l + jnp.sum(pu[0:1], axis=-1, keepdims=True)
        acc = acc + _dot(pu.astype(BF16), v_refs[u][0, 0].astype(BF16))[0:1]
    o_ref[0] = acc / l


def moba_decode_attention(page_table, ids, q, k_new, v_new, cache_k, cache_v, layer):
    s = page_table.shape[0]
    ppb = MOBA_BLOCK // PAGE_SIZE
    n_sel = ids.shape[2] * ppb

    def cmap(u):
        return lambda b, h, pt, ids_: (layer, pt[b, ids_[b, h, u // ppb] * ppb + u % ppb], 0, h)

    hmap = lambda b, h, pt, ids_: (b, 0, h)
    grid_spec = pltpu.PrefetchScalarGridSpec(
        num_scalar_prefetch=2,
        grid=(s, N_HEADS_MOBA),
        in_specs=[pl.BlockSpec((1, 1, HEAD_DIM), hmap)] * 3
        + [pl.BlockSpec((1, 1, PAGE_SIZE, HEAD_DIM), cmap(u)) for u in range(n_sel)] * 2,
        out_specs=pl.BlockSpec((1, 1, HEAD_DIM), hmap),
    )
    return pl.pallas_call(
        functools.partial(_moba_decode_kernel, n_pages_sel=n_sel, scale=HEAD_DIM ** -0.5),
        grid_spec=grid_spec,
        out_shape=jax.ShapeDtypeStruct((s, 1, N_HEADS_MOBA * HEAD_DIM), F32),
        compiler_params=_cparams("parallel", "arbitrary"),
        name="moba_decode_attention",
    )(page_table, ids, q, k_new, v_new, *([cache_k] * n_sel), *([cache_v] * n_sel))


def _page_copy(kidx_hbm, buf, sem, layer, page, slot):
    return pltpu.make_async_copy(kidx_hbm.at[layer, page], buf.at[slot], sem)


def _dsa_decode_select_kernel(pt_ref, qi_ref, wi_ref, kin_ref, kidx_hbm, out_ref, buf, s_ref, d_ref, sem,
                              *, layer, n_pages, n_keep):
    b = pl.program_id(0)
    ps = PAGE_SIZE

    def start(p, _):
        _page_copy(kidx_hbm, buf, sem, layer, pt_ref[b, p], p).start()
        return 0

    lax.fori_loop(0, n_pages, start, 0)
    qi = qi_ref[0].astype(BF16)
    wi = wi_ref[0]

    def wait(p, _):
        _page_copy(kidx_hbm, buf, sem, layer, 0, p).wait()
        return 0

    lax.fori_loop(0, n_pages, wait, 0)

    def score(p, _):
        d = _dot_nt(qi, buf[p].astype(BF16))
        s_ref[pl.ds(p, 1), :] = jnp.sum(wi * jnp.maximum(d, 0.0), axis=0, keepdims=True)
        return 0

    lax.fori_loop(0, n_pages, score, 0)
    d_new = jnp.sum(qi.astype(F32) * kin_ref[0].astype(BF16).astype(F32), axis=-1, keepdims=True)
    s_new = jnp.sum(wi * jnp.maximum(d_new, 0.0), axis=0, keepdims=True)

    sc = s_ref[...]
    row = lax.broadcasted_iota(jnp.int32, sc.shape, 0)
    col = lax.broadcasted_iota(jnp.int32, sc.shape, 1)
    idx = row * ps + col
    idx_new = n_pages * ps

    def total(a):
        return jnp.sum(jnp.sum(a, axis=1, keepdims=True), axis=0, keepdims=True)

    def count(pred_all, pred_new):
        return total(jnp.where(pred_all, 1.0, 0.0)) + jnp.where(pred_new, 1.0, 0.0)

    def bit_step(i, u):
        cand = u | jnp.left_shift(jnp.int32(1), 31 - i)
        thr = _key_to_f32(cand)
        return jnp.where(count(sc >= thr, s_new >= thr) >= n_keep, cand, u)

    thr = _key_to_f32(lax.fori_loop(0, 32, bit_step, jnp.zeros((1, 1), jnp.int32)))
    need = n_keep - count(sc > thr, s_new > thr)
    n_bits = int(idx_new).bit_length()

    def idx_step(i, x):
        cand = x | jnp.left_shift(jnp.int32(1), n_bits - 1 - i)
        below = count((sc == thr) & (idx < cand), (s_new == thr) & (idx_new < cand))
        return jnp.where(below < need, cand, x)

    cut = lax.fori_loop(0, n_bits, idx_step, jnp.zeros((1, 1), jnp.int32))
    sel = (sc > thr) | ((sc == thr) & (idx <= cut))
    sel_new = (s_new > thr) | ((s_new == thr) & (idx_new <= cut))

    selb = jnp.where(sel, 1.0, 0.0).astype(BF16)
    r2 = lax.broadcasted_iota(jnp.int32, (ps, ps), 0)
    c2 = lax.broadcasted_iota(jnp.int32, (ps, ps), 1)
    incl = _dot(selb, jnp.where(r2 <= c2, 1.0, 0.0).astype(BF16))
    rp = lax.broadcasted_iota(jnp.int32, (n_pages, n_pages), 0)
    cp = lax.broadcasted_iota(jnp.int32, (n_pages, n_pages), 1)
    page_off = _dot(jnp.where(cp < rp, 1.0, 0.0).astype(BF16), incl[:, ps - 1:ps].astype(BF16))
    d_ref[...] = jnp.where(sel, page_off + incl - 1.0, -1.0)

    slot = lax.broadcasted_iota(jnp.int32, (n_keep, ps), 0).astype(F32)
    lane = lax.broadcasted_iota(jnp.int32, (1, ps), 1)
    tok = jnp.where(c2 == 0, r2, jnp.where(c2 == 1, 1, 0)).astype(BF16)

    def gather(p, acc):
        onehot = jnp.where(d_ref[pl.ds(p, 1), :] == slot, 1.0, 0.0).astype(BF16)
        return acc + _dot(onehot, tok) * jnp.where(lane == 1, jnp.asarray(p, F32), 1.0)

    acc = lax.fori_loop(0, n_pages, gather, jnp.zeros((n_keep, ps), F32))
    found = acc[:, 0:1] + ps * acc[:, 1:2]
    last = lax.broadcasted_iota(jnp.int32, (n_keep, 1), 0) == n_keep - 1
    found = jnp.where(last & sel_new, float(idx_new), found)
    out_ref[0] = found.astype(jnp.int32)


def dsa_decode_select(page_table, qidx, widx, kidx_new, cache_kidx, layer):
    s, n_pages = page_table.shape
    n_keep = min(DSA_TOPK, max((n_pages * PAGE_SIZE + 1) // 4, 1))
    assert n_pages * PAGE_SIZE + 1 >= n_keep
    g = qidx.shape[1]
    grid_spec = pltpu.PrefetchScalarGridSpec(
        num_scalar_prefetch=1,
        grid=(s,),
        in_specs=[pl.BlockSpec((1, g, IDX_DIM), lambda b, pt: (b, 0, 0)),
                  pl.BlockSpec((1, g, 1), lambda b, pt: (b, 0, 0)),
                  pl.BlockSpec((1, 1, IDX_DIM), lambda b, pt: (b, 0, 0)),
                  pl.BlockSpec(memory_space=pl.ANY)],
        out_specs=pl.BlockSpec((1, n_keep, 1), lambda b, pt: (b, 0, 0)),
        scratch_shapes=[pltpu.VMEM((n_pages, PAGE_SIZE, IDX_DIM), F32),
                        pltpu.VMEM((n_pages, PAGE_SIZE), F32), pltpu.VMEM((n_pages, PAGE_SIZE), F32),
                        pltpu.SemaphoreType.DMA(())],
    )
    return pl.pallas_call(
        functools.partial(_dsa_decode_select_kernel, layer=layer, n_pages=n_pages, n_keep=n_keep),
        grid_spec=grid_spec,
        out_shape=jax.ShapeDtypeStruct((s, n_keep, 1), jnp.int32),
        compiler_params=_cparams("arbitrary"),
        name="dsa_decode_select",
    )(page_table, qidx, widx, kidx_new, cache_kidx)


def _row_copies(b, r, idx_ref, pt_ref, new_hbm, cache_hbm, buf, sem, layer, past_len):
    i = idx_ref[b, r]
    ic = jnp.minimum(i, past_len - 1)
    page = pt_ref[b, ic // PAGE_SIZE]
    from_cache = pltpu.make_async_copy(cache_hbm.at[layer, page, pl.ds(ic % PAGE_SIZE, 1), :],
                                       buf.at[pl.ds(r, 1), :], sem)
    from_new = pltpu.make_async_copy(new_hbm.at[b], buf.at[pl.ds(r, 1), :], sem)
    return i < past_len, from_cache, from_new


def _dsa_decode_attn_kernel(pt_ref, idx_ref, q_ref, kn_hbm, vn_hbm, ck_hbm, cv_hbm, o_ref, kbuf, vbuf, sems,
                            *, layer, n_keep, past_len, scale):
    b = pl.program_id(0)

    def start(r, _):
        for new_hbm, cache_hbm, buf, sem in ((kn_hbm, ck_hbm, kbuf, sems.at[0]), (vn_hbm, cv_hbm, vbuf, sems.at[1])):
            in_cache, from_cache, from_new = _row_copies(b, r, idx_ref, pt_ref, new_hbm, cache_hbm, buf, sem,
                                                         layer, past_len)

            @pl.when(in_cache)
            def _():
                from_cache.start()

            @pl.when(jnp.logical_not(in_cache))
            def _():
                from_new.start()
        return 0

    lax.fori_loop(0, n_keep, start, 0)

    def wait(r, _):
        for new_hbm, buf, sem in ((kn_hbm, kbuf, sems.at[0]), (vn_hbm, vbuf, sems.at[1])):
            pltpu.make_async_copy(new_hbm.at[b], buf.at[pl.ds(r, 1), :], sem).wait()
        return 0

    lax.fori_loop(0, n_keep, wait, 0)
    qbd = _block_diag_rows(q_ref[0], N_HEADS).astype(BF16)
    s = _dot_nt(qbd, kbuf[...].astype(BF16)) * scale
    m = jnp.max(s, axis=-1, keepdims=True)
    p = jnp.exp(s - m)
    l = jnp.sum(p, axis=-1, keepdims=True)
    o = _dot(p.astype(BF16), vbuf[...].astype(BF16)) / l
    o_ref[0] = _diag_blocks(o, N_HEADS)


def dsa_decode_attention(page_table, idx, q, k_new, v_new, cache_k, cache_v, layer):
    s, n_pages = page_table.shape
    n_keep = idx.shape[1]
    w = N_HEADS * HEAD_DIM
    grid_spec = pltpu.PrefetchScalarGridSpec(
        num_scalar_prefetch=2,
        grid=(s,),
        in_specs=[pl.BlockSpec((1, 1, w), lambda b, pt, ix: (b, 0, 0))] + [pl.BlockSpec(memory_space=pl.ANY)] * 4,
        out_specs=pl.BlockSpec((1, 1, w), lambda b, pt, ix: (b, 0, 0)),
        scratch_shapes=[pltpu.VMEM((n_keep, w), F32), pltpu.VMEM((n_keep, w), F32),
                        pltpu.SemaphoreType.DMA((2,))],
    )
    return pl.pallas_call(
        functools.partial(_dsa_decode_attn_kernel, layer=layer, n_keep=n_keep,
                          past_len=n_pages * PAGE_SIZE, scale=HEAD_DIM ** -0.5),
        grid_spec=grid_spec,
        out_shape=jax.ShapeDtypeStruct((s, 1, w), F32),
        compiler_params=_cparams("arbitrary"),
        name="dsa_decode_attention",
    )(page_table, idx, q, k_new, v_new, cache_k, cache_v)


def _rope_tables(pos, half, reps):
    inv = ROPE_THETA ** (-jnp.arange(half, dtype=F32) / half)
    ang = pos.astype(F32)[:, None] * inv[None, :]
    cos, sin = jnp.cos(ang), jnp.sin(ang)
    return jnp.tile(jnp.concatenate([cos, cos], axis=1), (1, reps)), \
        jnp.tile(jnp.concatenate([-sin, sin], axis=1), (1, reps))


def _idx_tables(pos):
    cos, sin = _rope_tables(pos, IDX_DIM // 2, IDX_HEADS + 1)
    n = pos.shape[0]
    pad = TAIL_WIDTH - cos.shape[1]
    return jnp.concatenate([cos, jnp.ones((n, pad), F32)], axis=1), \
        jnp.concatenate([sin, jnp.zeros((n, pad), F32)], axis=1)


TAIL_COLS = IDX_HEADS * IDX_DIM + IDX_DIM + IDX_HEADS
TAIL_WIDTH = -(-TAIL_COLS // LANES) * LANES


def _prompt_layer(x, i, pos, b, l, wts, tabs):
    (attn_norm, ffn_norm, w_in_even, w_out_even, w_in_odd, w_out_odd, w_tail, w_gate, w_up, w_down) = wts
    cos, sin, cos_t, sin_t = tabs
    d = x.shape[1]
    mw = N_HEADS * HEAD_DIM
    tm, tn = 512, 1024
    g = attn_norm[i][None, :]
    even = i % 2 == 0
    w_in = w_in_even[i // 2] if even else w_in_odd[i // 2]
    w_out = w_out_even[i // 2] if even else w_out_odd[i // 2]
    rope_q = (N_HEADS_MOBA * HEAD_DIM // tn) if even else mw // tn
    common = dict(tm=tm, tn=tn, half=HEAD_DIM // 2)
    (q16,) = norm_proj(x, g, w_in, cos, sin, col0=0, ncols=mw, rope_tiles=rope_q, want_bf16=True, **common)
    k_out = norm_proj(x, g, w_in, cos, sin, col0=mw, ncols=mw, rope_tiles=rope_q, want_f32=True,
                      want_bf16=True, want_kmean=even, **common)
    v32, v16 = norm_proj(x, g, w_in, cos, sin, col0=2 * mw, ncols=mw, rope_tiles=0, want_f32=True,
                         want_bf16=True, **common)
    k32, k16 = k_out[0], k_out[1]
    sh = (b, l, mw)
    if even:
        kmean = k_out[2].reshape(b, l // MOBA_BLOCK, mw)
        o = even_attention(q16.reshape(sh), k16.reshape(sh), v16.reshape(sh), kmean)
        kidx = None
    else:
        (tail,) = norm_proj(x, g, w_tail[i // 2], cos_t, sin_t, col0=0, ncols=TAIL_WIDTH, tm=tm,
                            tn=TAIL_WIDTH, rope_tiles=1, half=IDX_DIM // 2, want_f32=True)
        c1 = IDX_HEADS * IDX_DIM
        qidx = tail[:, :c1].reshape(b, l, IDX_HEADS, IDX_DIM).transpose(0, 2, 1, 3).astype(BF16)
        kidx = tail[:, c1:c1 + IDX_DIM].reshape(b, l, IDX_DIM)
        widx = tail[:, c1 + IDX_DIM:TAIL_COLS].reshape(b, l, IDX_HEADS)
        o = dsa_attention(q16.reshape(sh), k16.reshape(sh), v16.reshape(sh), qidx, kidx.astype(BF16), widx)
    x = matmul_res(o.reshape(b * l, mw), w_out, x, tm=tm, tn=tn)
    hmid = gate_up(x, ffn_norm[i][None, :], w_gate[i], w_up[i], tm=tm, tn=512)
    x = matmul_res(hmid, w_down[i], x, tm=tm, tn=512)
    return x, k32, v32, kidx


def _decode_layer(x, i, page_table, ck, cv, cache_kidx, wts, tabs):
    (attn_norm, ffn_norm, w_in_even, w_out_even, w_in_odd, w_out_odd, w_tail, w_gate, w_up, w_down) = wts
    cos, sin, cos_t, sin_t = tabs
    s = x.shape[0]
    mw = N_HEADS * HEAD_DIM
    tm, tn = s, 1024
    g = attn_norm[i][None, :]
    even = i % 2 == 0
    w_in = w_in_even[i // 2] if even else w_in_odd[i // 2]
    w_out = w_out_even[i // 2] if even else w_out_odd[i // 2]
    rope_q = (N_HEADS_MOBA * HEAD_DIM // tn) if even else mw // tn
    common = dict(tm=tm, tn=tn, half=HEAD_DIM // 2, want_f32=True)
    (q,) = norm_proj(x, g, w_in, cos, sin, col0=0, ncols=mw, rope_tiles=rope_q, **common)
    (k,) = norm_proj(x, g, w_in, cos, sin, col0=mw, ncols=mw, rope_tiles=rope_q, **common)
    (v,) = norm_proj(x, g, w_in, cos, sin, col0=2 * mw, ncols=mw, rope_tiles=0, **common)
    q3, k3, v3 = q.reshape(s, 1, mw), k.reshape(s, 1, mw), v.reshape(s, 1, mw)
    if even:
        ids, o_sb = even_decode_sweep(page_table, q3, ck, cv, i)
        o_moba = moba_decode_attention(page_table, ids[:, :, :MOBA_TOPK], q3, k3, v3, ck, cv, i)
        o = jnp.concatenate([o_moba, o_sb], axis=-1)
        kidx = None
    else:
        (tail,) = norm_proj(x, g, w_tail[i // 2], cos_t, sin_t, col0=0, ncols=TAIL_WIDTH, tm=tm,
                            tn=TAIL_WIDTH, rope_tiles=1, half=IDX_DIM // 2, want_f32=True)
        c1 = IDX_HEADS * IDX_DIM
        qidx = tail[:, :c1].reshape(s, IDX_HEADS, IDX_DIM)
        kidx = tail[:, c1:c1 + IDX_DIM].reshape(s, 1, IDX_DIM)
        widx = tail[:, c1 + IDX_DIM:TAIL_COLS].reshape(s, IDX_HEADS, 1)
        idx = dsa_decode_select(page_table, qidx, widx, kidx, cache_kidx, i // 2)
        o = dsa_decode_attention(page_table, idx[:, :, 0], q3, k3, v3, ck, cv, i)
    x = matmul_res(o.reshape(s, mw).astype(BF16), w_out, x, tm=tm, tn=tn)
    hmid = gate_up(x, ffn_norm[i][None, :], w_gate[i], w_up[i], tm=tm, tn=512)
    x = matmul_res(hmid, w_down[i], x, tm=tm, tn=512)
    return x, k, v, kidx


def kernel(x_prompt, x_sample, cache_k, cache_v, cache_kidx, page_table, attn_norm, ffn_norm, final_norm,
           w_in_even, w_out_even, w_in_odd, w_out_odd, w_gate, w_up, w_down):
    b, l, d = x_prompt.shape
    depth = attn_norm.shape[0]
    mw = N_HEADS * HEAD_DIM
    w_tail = jnp.pad(w_in_odd[:, :, 3 * mw:], ((0, 0), (0, 0), (0, TAIL_WIDTH - TAIL_COLS))).astype(BF16)
    wts = (attn_norm, ffn_norm, w_in_even.astype(BF16), w_out_even.astype(BF16),
           w_in_odd[:, :, :3 * mw].astype(BF16), w_out_odd.astype(BF16), w_tail,
           w_gate.astype(BF16), w_up.astype(BF16), w_down.astype(BF16))
    pos = jnp.arange(l, dtype=jnp.int32)
    tabs = _rope_tables(pos, HEAD_DIM // 2, 1) + _idx_tables(pos)

    x = x_prompt.reshape(b * l, d)
    ks, vs, kis = [], [], []
    for i in range(depth):
        x, k32, v32, kidx = _prompt_layer(x, i, pos, b, l, wts, tabs)
        ks.append(k32.reshape(b, l, N_HEADS, HEAD_DIM))
        vs.append(v32.reshape(b, l, N_HEADS, HEAD_DIM))
        if kidx is not None:
            kis.append(kidx)
    y_prompt = rmsnorm(x, final_norm[None, :], tm=512).reshape(b, l, d)

    db = x_sample.shape[0]
    assert x_sample.shape[1] == 1
    pool = cache_k.shape[1]
    past_len = page_table.shape[1] * PAGE_SIZE
    ck = cache_k.reshape(depth, pool, PAGE_SIZE, mw)
    cv = cache_v.reshape(depth, pool, PAGE_SIZE, mw)
    pos_d = jnp.full((db,), past_len, jnp.int32)
    tabs_d = _rope_tables(pos_d, HEAD_DIM // 2, 1) + _idx_tables(pos_d)
    xd = x_sample.reshape(db, d)
    kd, vd, kid = [], [], []
    for i in range(depth):
        xd, k32, v32, kidx = _decode_layer(xd, i, page_table, ck, cv, cache_kidx, wts, tabs_d)
        kd.append(k32.reshape(db, 1, N_HEADS, HEAD_DIM))
        vd.append(v32.reshape(db, 1, N_HEADS, HEAD_DIM))
        if kidx is not None:
            kid.append(kidx)
    y_sample = rmsnorm(xd, final_norm[None, :], tm=db).reshape(db, 1, d)
    return (y_prompt, y_sample, jnp.stack(ks), jnp.stack(vs), jnp.stack(kis),
            jnp.stack(kd), jnp.stack(vd), jnp.stack(kid))
```

```python
import functools

import jax
import jax.numpy as jnp
from jax import lax
from jax.experimental import pallas as pl
from jax.experimental.pallas import tpu as pltpu

F32 = jnp.float32
BF16 = jnp.bfloat16
NEG_INF = float("-inf")

HEAD_DIM = 128
N_HEADS = 16
N_HEADS_MOBA = 8
MOBA_BLOCK = 256
MOBA_TOPK = 3
IDX_HEADS = 8
IDX_DIM = 64
DSA_TOPK = 256
ROPE_THETA = 10000.0
RMS_EPS = 1e-6
PAGE_SIZE = 128

LOG2E = 1.4426950408889634
HEADS_PER_STEP = 4
LOOP_UNROLL = 8
SB_DEAD = -110.0
ATTN_CHUNK = 512
LANES = 128
VMEM_LIMIT = 56 * 1024 * 1024

NT_DIMS = (((1,), (1,)), ((), ()))


def _cparams(*sem):
    return pltpu.CompilerParams(dimension_semantics=sem, vmem_limit_bytes=VMEM_LIMIT)


def _dot(a, b):
    return jnp.dot(a, b, preferred_element_type=F32)


def _dot_nt(a, b):
    return lax.dot_general(a, b, NT_DIMS, preferred_element_type=F32)


def _split3(x):
    hi = x.astype(BF16)
    r1 = x - hi.astype(F32)
    mid = r1.astype(BF16)
    lo = (r1 - mid.astype(F32)).astype(BF16)
    return hi, mid, lo


def _split2(x):
    hi = x.astype(BF16)
    return hi, (x - hi.astype(F32)).astype(BF16)


def _rope_cols(y, cos_ref, sin_ref, half):
    tn = y.shape[1]
    wide = cos_ref.shape[1] == tn and tn != LANES
    outs = []
    lane = lax.broadcasted_iota(jnp.int32, (y.shape[0], LANES), 1)
    for s in range(tn // LANES):
        ys = y[:, s * LANES:(s + 1) * LANES]
        if wide:
            c = cos_ref[:, s * LANES:(s + 1) * LANES]
            sn = sin_ref[:, s * LANES:(s + 1) * LANES]
        else:
            c = cos_ref[...]
            sn = sin_ref[...]
        if 2 * half == LANES:
            partner = pltpu.roll(ys, half, 1)
        else:
            partner = jnp.where(lane % (2 * half) < half,
                                pltpu.roll(ys, LANES - half, 1), pltpu.roll(ys, half, 1))
        outs.append(ys * c + partner * sn)
    return jnp.concatenate(outs, axis=1) if len(outs) > 1 else outs[0]


def _norm_proj_kernel(x_ref, g_ref, w_ref, cos_ref, sin_ref, *refs, n_col_tiles, rope_tiles, half,
                      want_f32, want_bf16, want_kmean, has_colscale):
    refs = list(refs)
    hn_ref = refs.pop()
    cs_ref = refs.pop(0) if has_colscale else None
    o32_ref = refs.pop(0) if want_f32 else None
    o16_ref = refs.pop(0) if want_bf16 else None
    km_ref = refs.pop(0) if want_kmean else None
    j = pl.program_id(1)

    @pl.when(j == 0)
    def _():
        x = x_ref[...]
        ms = jnp.mean(x * x, axis=-1, keepdims=True)
        hn_ref[...] = (x * lax.rsqrt(ms + RMS_EPS) * g_ref[...]).astype(BF16)

    acc = _dot(hn_ref[...], w_ref[...])

    def emit(y):
        if has_colscale:
            y = y * cs_ref[...]
        if want_f32:
            o32_ref[...] = y
        if want_bf16:
            o16_ref[...] = y.astype(BF16)
        if want_kmean:
            for r in range(y.shape[0] // MOBA_BLOCK):
                km_ref[r] = jnp.mean(y[r * MOBA_BLOCK:(r + 1) * MOBA_BLOCK], axis=0, keepdims=True)

    if rope_tiles == 0:
        emit(acc)
    elif rope_tiles >= n_col_tiles:
        emit(_rope_cols(acc, cos_ref, sin_ref, half))
    else:
        @pl.when(j < rope_tiles)
        def _():
            emit(_rope_cols(acc, cos_ref, sin_ref, half))

        @pl.when(j >= rope_tiles)
        def _():
            emit(acc)


def norm_proj(x, g, w, cos, sin, *, col0, ncols, tm, tn, rope_tiles, half,
              want_f32=False, want_bf16=False, want_kmean=False, colscale=None):
    t, d = x.shape
    assert t % tm == 0 and ncols % tn == 0 and col0 % tn == 0 and cos.shape[0] % tm == 0
    pos_blocks = cos.shape[0] // tm
    cb0 = col0 // tn
    n_col_tiles = ncols // tn
    out_shape, out_specs = [], []
    if want_f32:
        out_shape.append(jax.ShapeDtypeStruct((t, ncols), F32))
        out_specs.append(pl.BlockSpec((tm, tn), lambda i, j: (i, j)))
    if want_bf16:
        out_shape.append(jax.ShapeDtypeStruct((t, ncols), BF16))
        out_specs.append(pl.BlockSpec((tm, tn), lambda i, j: (i, j)))
    if want_kmean:
        assert tm % MOBA_BLOCK == 0
        out_shape.append(jax.ShapeDtypeStruct((t // MOBA_BLOCK, 1, ncols), F32))
        out_specs.append(pl.BlockSpec((tm // MOBA_BLOCK, 1, tn), lambda i, j: (i, 0, j)))
    kern = functools.partial(_norm_proj_kernel, n_col_tiles=n_col_tiles, rope_tiles=rope_tiles,
                             half=half, want_f32=want_f32, want_bf16=want_bf16, want_kmean=want_kmean,
                             has_colscale=colscale is not None)
    tw = cos.shape[1]
    extra_in = [] if colscale is None else [colscale]
    extra_specs = [] if colscale is None else [pl.BlockSpec((1, tn), lambda i, j: (0, j))]
    return pl.pallas_call(
        kern,
        grid=(t // tm, n_col_tiles),
        in_specs=[
            pl.BlockSpec((tm, d), lambda i, j: (i, 0)),
            pl.BlockSpec((1, d), lambda i, j: (0, 0)),
            pl.BlockSpec((d, tn), lambda i, j: (0, cb0 + j)),
            pl.BlockSpec((tm, tw), lambda i, j: (i % pos_blocks, 0)),
            pl.BlockSpec((tm, tw), lambda i, j: (i % pos_blocks, 0)),
        ] + extra_specs,
        out_specs=out_specs,
        out_shape=out_shape,
        scratch_shapes=[pltpu.VMEM((tm, d), BF16)],
        compiler_params=_cparams("parallel", "arbitrary"),
        name="norm_proj",
    )(x, g, w, cos, sin, *extra_in)


def _matmul_res_kernel(a_ref, w_ref, r_ref, o_ref):
    o_ref[...] = r_ref[...] + _dot(a_ref[...], w_ref[...])


def matmul_res(a, w, res, *, tm, tn):
    t, k = a.shape
    n = w.shape[1]
    assert t % tm == 0 and n % tn == 0
    return pl.pallas_call(
        _matmul_res_kernel,
        grid=(t // tm, n // tn),
        in_specs=[
            pl.BlockSpec((tm, k), lambda i, j: (i, 0)),
            pl.BlockSpec((k, tn), lambda i, j: (0, j)),
            pl.BlockSpec((tm, tn), lambda i, j: (i, j)),
        ],
        out_specs=pl.BlockSpec((tm, tn), lambda i, j: (i, j)),
        out_shape=jax.ShapeDtypeStruct((t, n), F32),
        compiler_params=_cparams("parallel", "arbitrary"),
        name="matmul_res",
    )(a, w, res)


def _gate_up_kernel(x_ref, g_ref, wg_ref, wu_ref, o_ref, hn_ref):
    @pl.when(pl.program_id(1) == 0)
    def _():
        x = x_ref[...]
        ms = jnp.mean(x * x, axis=-1, keepdims=True)
        hn_ref[...] = (x * lax.rsqrt(ms + RMS_EPS) * g_ref[...]).astype(BF16)

    hn = hn_ref[...]
    a = _dot(hn, wg_ref[...])
    u = _dot(hn, wu_ref[...])
    o_ref[...] = (a / (1.0 + jnp.exp(-a)) * u).astype(BF16)


def gate_up(x, g, wg, wu, *, tm, tn):
    t, d = x.shape
    f = wg.shape[1]
    assert t % tm == 0 and f % tn == 0
    return pl.pallas_call(
        _gate_up_kernel,
        grid=(t // tm, f // tn),
        in_specs=[
            pl.BlockSpec((tm, d), lambda i, j: (i, 0)),
            pl.BlockSpec((1, d), lambda i, j: (0, 0)),
            pl.BlockSpec((d, tn), lambda i, j: (0, j)),
            pl.BlockSpec((d, tn), lambda i, j: (0, j)),
        ],
        out_specs=pl.BlockSpec((tm, tn), lambda i, j: (i, j)),
        out_shape=jax.ShapeDtypeStruct((t, f), BF16),
        scratch_shapes=[pltpu.VMEM((tm, d), BF16)],
        compiler_params=_cparams("parallel", "arbitrary"),
        name="gate_up",
    )(x, g, wg, wu)


def _rmsnorm_kernel(x_ref, g_ref, o_ref):
    x = x_ref[...]
    ms = jnp.mean(x * x, axis=-1, keepdims=True)
    o_ref[...] = x * lax.rsqrt(ms + RMS_EPS) * g_ref[...]


def rmsnorm(x, g, *, tm):
    t, d = x.shape
    assert t % tm == 0
    return pl.pallas_call(
        _rmsnorm_kernel,
        grid=(t // tm,),
        in_specs=[pl.BlockSpec((tm, d), lambda i: (i, 0)), pl.BlockSpec((1, d), lambda i: (0, 0))],
        out_specs=pl.BlockSpec((tm, d), lambda i: (i, 0)),
        out_shape=jax.ShapeDtypeStruct((t, d), F32),
        compiler_params=_cparams("parallel"),
        name="rmsnorm",
    )(x, g)


def _fold_lanes(x, op):
    r = x[:, :LANES]
    for i in range(1, x.shape[1] // LANES):
        r = op(r, x[:, i * LANES:(i + 1) * LANES])
    return r


def _head_cols(g):
    return slice(g * HEAD_DIM, (g + 1) * HEAD_DIM)


def _biased_attention(qs, k_ref, v_ref, bias, s_refs, n_chunks, ck):
    tq = qs[0].shape[0]
    heads = range(len(qs))

    def row_max(c, mvecs):
        k0 = pl.multiple_of(c * ck, ck)
        out = []
        for g in heads:
            s = _dot_nt(qs[g], k_ref[0, pl.ds(k0, ck), _head_cols(g)]) + bias(g, k0)
            s_refs[g][:, pl.ds(k0, ck)] = s
            out.append(jnp.maximum(mvecs[g], _fold_lanes(s, jnp.maximum)))
        return tuple(out)

    mvecs = lax.fori_loop(0, n_chunks, row_max, tuple(jnp.full((tq, LANES), NEG_INF, F32) for _ in heads))
    ms = [jnp.max(mv, axis=-1, keepdims=True) for mv in mvecs]

    def accumulate(c, carry):
        k0 = pl.multiple_of(c * ck, ck)
        out = []
        for g in heads:
            lvec, acc = carry[g]
            p = jnp.exp2(s_refs[g][:, pl.ds(k0, ck)] - ms[g])
            out.append((lvec + _fold_lanes(p, jnp.add),
                        acc + _dot(p.astype(BF16), v_ref[0, pl.ds(k0, ck), _head_cols(g)])))
        return tuple(out)

    init = tuple((jnp.zeros((tq, LANES), F32), jnp.zeros((tq, HEAD_DIM), F32)) for _ in heads)
    res = lax.fori_loop(0, n_chunks, accumulate, init)
    return [acc / jnp.sum(lvec, axis=-1, keepdims=True) for lvec, acc in res]


def _moba_bias(q, km, b_ref, qi):
    tq = q.shape[0]
    blk = MOBA_BLOCK
    nb = km.shape[0]
    km_hi, km_mid, km_lo = _split3(km)
    gate = _dot_nt(q, km_hi) + _dot_nt(q, km_mid) + _dot_nt(q, km_lo)
    n_idx = lax.broadcasted_iota(jnp.int32, (tq, nb), 1)
    g = jnp.where(n_idx < qi, gate, NEG_INF)
    sel = jnp.zeros((tq, nb), dtype=jnp.bool_)
    for _ in range(min(MOBA_TOPK, nb)):
        mx = jnp.max(g, axis=-1, keepdims=True)
        first = jnp.min(jnp.where(g == mx, n_idx, nb), axis=-1, keepdims=True)
        hit = n_idx == first
        sel = sel | (hit & (mx > NEG_INF))
        g = jnp.where(hit, NEG_INF, g)
    bias = jnp.where(sel, 0.0, NEG_INF)

    def fill(j, _):
        bcol = jnp.max(jnp.where(n_idx == j, bias, NEG_INF), axis=-1, keepdims=True)
        b_ref[:, pl.ds(pl.multiple_of(j * blk, blk), blk)] = jnp.broadcast_to(bcol, (tq, blk))
        return 0

    lax.fori_loop(0, qi, fill, 0)
    row = lax.broadcasted_iota(jnp.int32, (tq, blk), 0)
    col = lax.broadcasted_iota(jnp.int32, (tq, blk), 1)
    b_ref[:, pl.ds(pl.multiple_of(qi * blk, blk), blk)] = jnp.where(col <= row, 0.0, NEG_INF)

    @pl.when(qi + 1 < nb)
    def _():
        b_ref[:, pl.ds(pl.multiple_of((qi + 1) * blk, blk), blk)] = jnp.full((tq, blk), NEG_INF, F32)


def _moba_tile(q_ref, k_ref, v_ref, km_ref, o_ref, b_ref, qi):
    n = q_ref.shape[2] // HEAD_DIM
    qs = [q_ref[0, :, _head_cols(g)] for g in range(n)]
    for g in range(n):
        _moba_bias(qs[g], km_ref[0, :, _head_cols(g)], b_ref.at[g], qi)
    ck = min(ATTN_CHUNK, k_ref.shape[1])
    n_chunks = ((qi + 1) * MOBA_BLOCK + ck - 1) // ck
    outs = _biased_attention(qs, k_ref, v_ref, lambda g, k0: b_ref[g, :, pl.ds(k0, ck)],
                             [b_ref.at[g] for g in range(n)], n_chunks, ck)
    for g in range(n):
        o_ref[0, :, _head_cols(g)] = outs[g].astype(o_ref.dtype)


def _sb_block(q, k, v, tri2, carry, acc, before):
    z = _dot_nt(q, k)
    ls = jnp.minimum(z, 0.0) - jnp.log(1.0 + jnp.exp(-jnp.abs(z)))
    lk = ls - z
    if before is not None:
        lk = jnp.where(before, lk, 0.0)
    cs = _dot(jnp.concatenate(_split2(lk), axis=1), tri2)
    w = jnp.exp(ls + (carry + cs))
    if before is not None:
        w = jnp.where(before, w, 0.0)
    acc = acc + _dot(w.astype(BF16), v)
    carry = carry + cs[:, 0:1] + lk[:, 0:1]
    return carry, acc


def _sb_alive(carries):
    m = jnp.max(carries[0])
    for c in carries[1:]:
        m = jnp.maximum(m, jnp.max(c))
    return m > SB_DEAD


def _tri2(blk):
    row = lax.broadcasted_iota(jnp.int32, (2 * blk, blk), 0)
    col = lax.broadcasted_iota(jnp.int32, (2 * blk, blk), 1)
    return jnp.where(row % blk > col, 1.0, 0.0).astype(BF16)


def _sb_tile(q_ref, k_ref, v_ref, o_ref, qi):
    tq = q_ref.shape[1]
    blk = tq
    n = q_ref.shape[2] // HEAD_DIM
    heads = range(n)
    qs = [q_ref[0, :, _head_cols(g)] for g in heads]
    row = lax.broadcasted_iota(jnp.int32, (tq, blk), 0)
    col = lax.broadcasted_iota(jnp.int32, (tq, blk), 1)
    tri2 = _tri2(blk)

    def blocks(k0, state, before):
        return tuple(_sb_block(qs[g], k_ref[0, pl.ds(k0, blk), _head_cols(g)],
                               v_ref[0, pl.ds(k0, blk), _head_cols(g)], tri2, *state[g], before) for g in heads)

    init = tuple((jnp.zeros((tq, 1), F32), jnp.zeros((tq, HEAD_DIM), F32)) for _ in heads)
    state = blocks(pl.multiple_of(qi * blk, blk), init, col < row)

    def live(c):
        return (c[0] < qi) & _sb_alive([st[0] for st in c[1]])

    def step(c):
        jj, st = c
        return jj + 1, blocks(pl.multiple_of((qi - 1 - jj) * blk, blk), st, None)

    _, state = lax.while_loop(live, step, (jnp.int32(0), state))
    for g in heads:
        o_ref[0, :, _head_cols(g)] = state[g][1].astype(o_ref.dtype)


def _even_attn_kernel(q_ref, k_ref, v_ref, km_ref, o_ref, b_ref, *, n_moba_groups):
    hg = pl.program_id(1)
    qi = pl.program_id(2)

    @pl.when(hg < n_moba_groups)
    def _():
        _moba_tile(q_ref, k_ref, v_ref, km_ref, o_ref, b_ref, qi)

    @pl.when(hg >= n_moba_groups)
    def _():
        _sb_tile(q_ref, k_ref, v_ref, o_ref, qi)


def even_attention(q, k, v, kmean):
    b, l, w = q.shape
    nh = w // HEAD_DIM
    tq = MOBA_BLOCK
    gw = HEADS_PER_STEP * HEAD_DIM
    assert l % tq == 0 and nh == N_HEADS and N_HEADS_MOBA % HEADS_PER_STEP == 0
    nb = l // MOBA_BLOCK
    return pl.pallas_call(
        functools.partial(_even_attn_kernel, n_moba_groups=N_HEADS_MOBA // HEADS_PER_STEP),
        grid=(b, nh // HEADS_PER_STEP, l // tq),
        in_specs=[
            pl.BlockSpec((1, tq, gw), lambda bi, h, i: (bi, i, h)),
            pl.BlockSpec((1, l, gw), lambda bi, h, i: (bi, 0, h)),
            pl.BlockSpec((1, l, gw), lambda bi, h, i: (bi, 0, h)),
            pl.BlockSpec((1, nb, gw), lambda bi, h, i: (bi, 0, h)),
        ],
        out_specs=pl.BlockSpec((1, tq, gw), lambda bi, h, i: (bi, i, h)),
        out_shape=jax.ShapeDtypeStruct((b, l, w), BF16),
        scratch_shapes=[pltpu.VMEM((HEADS_PER_STEP, tq, l), F32)],
        compiler_params=_cparams("parallel", "parallel", "arbitrary"),
        name="even_attention",
    )(q, k, v, kmean)


def _key_to_f32(u):
    t = u ^ jnp.int32(-2 ** 31)
    bits = t ^ ((t >> 31) & jnp.int32(0x7FFFFFFF))
    return lax.bitcast_convert_type(bits, F32)


def _dsa_build_bias(qi_ref, ki_ref, wi_ref, s_ref, t, n_keep):
    tq = s_ref.shape[0]
    ck = tq
    n_chunks = t + 1
    row = lax.broadcasted_iota(jnp.int32, (tq, ck), 0)
    col = lax.broadcasted_iota(jnp.int32, (tq, ck), 1)
    wi = wi_ref[0]

    def score_chunk(c, _):
        k0 = pl.multiple_of(c * ck, ck)
        kc = ki_ref[0, pl.ds(k0, ck), :]
        sc = jnp.zeros((tq, ck), F32)
        for g in range(IDX_HEADS):
            sc = sc + wi[:, g:g + 1] * jnp.maximum(_dot_nt(qi_ref[0, g], kc), 0.0)
        s_ref[:, pl.ds(k0, ck)] = jnp.where((c < t) | (col <= row), sc, NEG_INF)
        return 0

    lax.fori_loop(0, n_chunks, score_chunk, 0)

    def count(pred):
        def body(c, acc):
            k0 = pl.multiple_of(c * ck, ck)
            return acc + _fold_lanes(jnp.where(pred(s_ref[:, pl.ds(k0, ck)], c), 1.0, 0.0), jnp.add)
        acc = lax.fori_loop(0, n_chunks, body, jnp.zeros((tq, LANES), F32))
        return jnp.sum(acc, axis=-1, keepdims=True)

    def bit_step(i, u):
        cand = u | jnp.left_shift(jnp.int32(1), 31 - i)
        thr = _key_to_f32(cand)
        return jnp.where(count(lambda s, c: s >= thr) >= n_keep, cand, u)

    u = lax.fori_loop(0, 32, bit_step, jnp.zeros((tq, 1), jnp.int32))
    thr = _key_to_f32(u)
    pos = t * tq + lax.broadcasted_iota(jnp.int32, (tq, 1), 0)
    keep_all = pos < n_keep
    n_ge = count(lambda s, c: s >= thr)
    tied = (n_ge > n_keep) & jnp.logical_not(keep_all)
    cut_ref_val = jnp.full((tq, 1), 2 ** 30, jnp.int32)

    def tie_break():
        need = n_keep - count(lambda s, c: s > thr)
        n_bits = max(1, int((s_ref.shape[1] - 1)).bit_length())

        def idx_step(i, x):
            cand = x | jnp.left_shift(jnp.int32(1), n_bits - 1 - i)
            below = count(lambda s, c: (s == thr) & (c * ck + col < cand))
            return jnp.where(below < need, cand, x)

        x = lax.fori_loop(0, n_bits, idx_step, jnp.zeros((tq, 1), jnp.int32))
        return jnp.where(tied, x, cut_ref_val)

    any_tied = jnp.max(jnp.where(tied, 1, 0)) > 0
    cut = lax.cond(any_tied, tie_break, lambda: cut_ref_val)

    def to_bias(c, _):
        k0 = pl.multiple_of(c * ck, ck)
        s = s_ref[:, pl.ds(k0, ck)]
        kept = (s > thr) | ((s == thr) & (c * ck + col <= cut)) | keep_all
        s_ref[:, pl.ds(k0, ck)] = jnp.where(kept & (s > NEG_INF), 0.0, NEG_INF)
        return 0

    lax.fori_loop(0, n_chunks, to_bias, 0)

    @pl.when(n_chunks * ck < s_ref.shape[1])
    def _():
        s_ref[:, pl.ds(pl.multiple_of(n_chunks * ck, ck), ck)] = jnp.full((tq, ck), NEG_INF, F32)


def _dsa_attn_kernel(q_ref, k_ref, v_ref, qi_ref, ki_ref, wi_ref, o_ref, s_ref, hs_ref, *, n_keep):
    t = pl.program_id(1)
    h = pl.program_id(2)
    tq = q_ref.shape[1]

    @pl.when(h == 0)
    def _():
        _dsa_build_bias(qi_ref, ki_ref, wi_ref, s_ref, t, n_keep)

    n = q_ref.shape[2] // HEAD_DIM
    qs = [q_ref[0, :, _head_cols(g)] for g in range(n)]
    ck = min(ATTN_CHUNK, k_ref.shape[1])
    n_chunks = ((t + 1) * tq + ck - 1) // ck
    outs = _biased_attention(qs, k_ref, v_ref, lambda g, k0: s_ref[:, pl.ds(k0, ck)],
                             [hs_ref.at[g] for g in range(n)], n_chunks, ck)
    for g in range(n):
        o_ref[0, :, _head_cols(g)] = outs[g].astype(o_ref.dtype)


def dsa_attention(q, k, v, qidx, kidx, widx):
    b, l, w = q.shape
    nh = w // HEAD_DIM
    tq = 256
    assert l % tq == 0
    n_keep = min(DSA_TOPK, max(l // 4, 1))
    g = qidx.shape[1]
    gw = HEADS_PER_STEP * HEAD_DIM
    assert nh % HEADS_PER_STEP == 0
    return pl.pallas_call(
        functools.partial(_dsa_attn_kernel, n_keep=n_keep),
        grid=(b, l // tq, nh // HEADS_PER_STEP),
        in_specs=[
            pl.BlockSpec((1, tq, gw), lambda bi, i, h: (bi, i, h)),
            pl.BlockSpec((1, l, gw), lambda bi, i, h: (bi, 0, h)),
            pl.BlockSpec((1, l, gw), lambda bi, i, h: (bi, 0, h)),
            pl.BlockSpec((1, g, tq, IDX_DIM), lambda bi, i, h: (bi, 0, i, 0)),
            pl.BlockSpec((1, l, IDX_DIM), lambda bi, i, h: (bi, 0, 0)),
            pl.BlockSpec((1, tq, g), lambda bi, i, h: (bi, i, 0)),
        ],
        out_specs=pl.BlockSpec((1, tq, gw), lambda bi, i, h: (bi, i, h)),
        out_shape=jax.ShapeDtypeStruct((b, l, w), BF16),
        scratch_shapes=[pltpu.VMEM((tq, l), F32), pltpu.VMEM((HEADS_PER_STEP, tq, l), F32)],
        compiler_params=_cparams("parallel", "arbitrary", "arbitrary"),
        name="dsa_attention",
    )(q, k, v, qidx, kidx, widx)


def _only_row(x, h):
    row = lax.broadcasted_iota(jnp.int32, x.shape, 0)
    return jnp.where(row == h, x, jnp.zeros_like(x))


def _head_rows(ref2d, h, n):
    return ref2d[pl.ds(h, n, stride=N_HEADS), :]


def _head_scores(q16, head_slice, heads):
    out = None
    for i, h in enumerate(heads):
        part = _dot_nt(_only_row(q16, i), head_slice(h).astype(BF16))
        out = part if out is None else out + part
    return out


def _head_mix(p16, head_slice, heads):
    out = None
    for i, h in enumerate(heads):
        part = _dot(_only_row(p16, i), head_slice(h).astype(BF16))
        out = part if out is None else out + part
    return out


def _moba_gate_kernel(pt_ref, q_ref, *refs, pg, n_groups):
    k_refs = refs[:pg]
    ids_ref, km_ref = refs[pg:]
    p = pl.program_id(1)
    na = N_HEADS_MOBA
    pages_per_block = MOBA_BLOCK // PAGE_SIZE
    ksum = None
    for u in range(pg):
        part = jnp.sum(k_refs[u][0, 0], axis=0)
        ksum = part if ksum is None else ksum + part
        if (u + 1) % pages_per_block == 0:
            km_ref[p * (pg // pages_per_block) + u // pages_per_block] = ksum * (1.0 / MOBA_BLOCK)
            ksum = None

    @pl.when(p == n_groups - 1)
    def _():
        nb = km_ref.shape[0]
        gate = jnp.sum(km_ref[...] * q_ref[0, 0:na, :][None], axis=-1)
        n_idx = lax.broadcasted_iota(jnp.int32, gate.shape, 0)
        ids_ref[0] = jnp.zeros(ids_ref.shape[1:], jnp.int32)
        for r in range(MOBA_TOPK):
            mx = jnp.max(gate, axis=0, keepdims=True)
            first = jnp.min(jnp.where(gate == mx, n_idx, nb), axis=0, keepdims=True)
            ids_ref[0, r:r + 1, 0:na] = first
            gate = jnp.where(n_idx == first, NEG_INF, gate)


def moba_decode_gate(page_table, q, cache_k, layer, pg=8):
    s, n_pages = page_table.shape
    assert n_pages % pg == 0 and pg % (MOBA_BLOCK // PAGE_SIZE) == 0
    n_groups = n_pages // pg
    nb = n_pages * PAGE_SIZE // MOBA_BLOCK
    na = N_HEADS_MOBA
    assert nb >= MOBA_TOPK and na == 8

    def kmap(u):
        return lambda b, p, pt: (layer, pt[b, p * pg + u], 0, 0, 0)

    grid_spec = pltpu.PrefetchScalarGridSpec(
        num_scalar_prefetch=1,
        grid=(s, n_groups),
        in_specs=[pl.BlockSpec((1, N_HEADS, HEAD_DIM), lambda b, p, pt: (b, 0, 0))]
        + [pl.BlockSpec((1, 1, PAGE_SIZE, na, HEAD_DIM), kmap(u)) for u in range(pg)],
        out_specs=pl.BlockSpec((1, 8, LANES), lambda b, p, pt: (b, 0, 0)),
        scratch_shapes=[pltpu.VMEM((nb, na, HEAD_DIM), F32)],
    )
    return pl.pallas_call(
        functools.partial(_moba_gate_kernel, pg=pg, n_groups=n_groups),
        grid_spec=grid_spec,
        out_shape=jax.ShapeDtypeStruct((s, 8, LANES), jnp.int32),
        compiler_params=_cparams("parallel", "arbitrary"),
        name="moba_decode_gate",
    )(page_table, q, *([cache_k] * pg))


def _sb_page_copies(pt_ref, ck_hbm, cv_hbm, kbuf, vbuf, sems, layer, b, page_idx, slot):
    page = pt_ref[b, page_idx]
    return (pltpu.make_async_copy(ck_hbm.at[layer, page], kbuf.at[slot], sems.at[0, slot]),
            pltpu.make_async_copy(cv_hbm.at[layer, page], vbuf.at[slot], sems.at[1, slot]))


def _sb_decode_kernel(pt_ref, q_ref, ck_hbm, cv_hbm, o_ref, kbuf, vbuf, sems, *, layer, n_pages, scale):
    b = pl.program_id(0)
    na = N_HEADS_MOBA
    nsb = N_HEADS - na
    tri2 = _tri2(PAGE_SIZE)
    qsb = (q_ref[0, na:, :] * scale).astype(BF16)

    def copies(i, slot):
        return _sb_page_copies(pt_ref, ck_hbm, cv_hbm, kbuf, vbuf, sems, layer, b, n_pages - 1 - i, slot)

    for cp in copies(0, 0):
        cp.start()

    def live(c):
        return (c[0] < n_pages) & _sb_alive([c[1]])

    def step(c):
        i, carry, acc = c
        slot = i % 2
        for cp in copies(i, slot):
            cp.wait()

        @pl.when(i + 1 < n_pages)
        def _():
            for cp in copies(i + 1, 1 - slot):
                cp.start()

        k2, v2 = kbuf.at[slot], vbuf.at[slot]
        z = _head_scores(qsb, lambda h: _head_rows(k2, na + h, PAGE_SIZE), range(nsb))
        ls = jnp.minimum(z, 0.0) - jnp.log(1.0 + jnp.exp(-jnp.abs(z)))
        lk = ls - z
        cs = _dot(jnp.concatenate(_split2(lk), axis=1), tri2)
        w = jnp.exp(ls + (carry + cs))
        acc = acc + _head_mix(w.astype(BF16), lambda h: _head_rows(v2, na + h, PAGE_SIZE), range(nsb))
        return i + 1, carry + cs[:, 0:1] + lk[:, 0:1], acc

    n_done, _, acc = lax.while_loop(live, step, (jnp.int32(0), jnp.zeros((nsb, 1), F32),
                                                 jnp.zeros((nsb, HEAD_DIM), F32)))

    @pl.when(n_done < n_pages)
    def _():
        for cp in copies(n_done, n_done % 2):
            cp.wait()

    o_ref[0] = acc


def sb_decode_attention(page_table, q, cache_k, cache_v, layer):
    s, n_pages = page_table.shape
    nsb = N_HEADS - N_HEADS_MOBA
    rows = PAGE_SIZE * N_HEADS
    grid_spec = pltpu.PrefetchScalarGridSpec(
        num_scalar_prefetch=1,
        grid=(s,),
        in_specs=[pl.BlockSpec((1, N_HEADS, HEAD_DIM), lambda b, pt: (b, 0, 0)),
                  pl.BlockSpec(memory_space=pl.ANY), pl.BlockSpec(memory_space=pl.ANY)],
        out_specs=pl.BlockSpec((1, nsb, HEAD_DIM), lambda b, pt: (b, 0, 0)),
        scratch_shapes=[pltpu.VMEM((2, rows, HEAD_DIM), F32), pltpu.VMEM((2, rows, HEAD_DIM), F32),
                        pltpu.SemaphoreType.DMA((2, 2))],
    )
    return pl.pallas_call(
        functools.partial(_sb_decode_kernel, layer=layer, n_pages=n_pages, scale=HEAD_DIM ** -0.5),
        grid_spec=grid_spec,
        out_shape=jax.ShapeDtypeStruct((s, nsb, HEAD_DIM), F32),
        compiler_params=_cparams("arbitrary"),
        name="sb_decode_attention",
    )(page_table, q, cache_k, cache_v)


def _moba_page_copy(cache_hbm, buf, sem, layer, page, h, u):
    return pltpu.make_async_copy(cache_hbm.at[layer, page, :, h], buf.at[h, u], sem)


def _moba_decode_kernel(pt_ref, ids_ref, q_ref, kn_ref, vn_ref, ck_hbm, cv_hbm, o_ref, kbuf, vbuf, sems,
                        *, layer, n_sel, scale):
    b = pl.program_id(0)
    ppb = MOBA_BLOCK // PAGE_SIZE
    na = N_HEADS_MOBA
    for h in range(na):
        for u in range(n_sel):
            page = pt_ref[b, ids_ref[b, h, u // ppb] * ppb + u % ppb]
            _moba_page_copy(ck_hbm, kbuf, sems.at[0], layer, page, h, u).start()
            _moba_page_copy(cv_hbm, vbuf, sems.at[1], layer, page, h, u).start()
    for h in range(na):
        for u in range(n_sel):
            _moba_page_copy(ck_hbm, kbuf, sems.at[0], layer, 0, h, u).wait()
            _moba_page_copy(cv_hbm, vbuf, sems.at[1], layer, 0, h, u).wait()

    q = q_ref[0]
    qr = q.astype(BF16).astype(F32)
    knew = kn_ref[0].astype(BF16).astype(F32)
    vnew = vn_ref[0].astype(BF16).astype(F32)
    s_own_all = jnp.sum(qr * knew, axis=-1, keepdims=True) * scale
    for h in range(na):
        q8 = jnp.broadcast_to(q[h:h + 1], (8, HEAD_DIM)).astype(BF16)
        s_own = s_own_all[h:h + 1]
        scores = [_dot_nt(q8, kbuf[h, u].astype(BF16))[0:1] * scale for u in range(n_sel)]
        m = s_own
        for sc in scores:
            m = jnp.maximum(m, jnp.max(sc, axis=-1, keepdims=True))
        p_own = jnp.exp(s_own - m)
        l = p_own
        acc = p_own * vnew[h:h + 1]
        for u, sc in enumerate(scores):
            pu = jnp.exp(sc - m)
            l = l + jnp.sum(pu, axis=-1, keepdims=True)
            acc = acc + _dot(jnp.broadcast_to(pu, (8, PAGE_SIZE)).astype(BF16), vbuf[h, u].astype(BF16))[0:1]
        o_ref[0, h:h + 1, :] = acc / l


def moba_decode_attention(page_table, ids, q, k_new, v_new, cache_k, cache_v, layer):
    s = page_table.shape[0]
    ppb = MOBA_BLOCK // PAGE_SIZE
    n_sel = ids.shape[2] * ppb
    na = N_HEADS_MOBA
    hspec = pl.BlockSpec((1, N_HEADS, HEAD_DIM), lambda b, pt, ids_: (b, 0, 0))
    grid_spec = pltpu.PrefetchScalarGridSpec(
        num_scalar_prefetch=2,
        grid=(s,),
        in_specs=[hspec, hspec, hspec, pl.BlockSpec(memory_space=pl.ANY), pl.BlockSpec(memory_space=pl.ANY)],
        out_specs=pl.BlockSpec((1, na, HEAD_DIM), lambda b, pt, ids_: (b, 0, 0)),
        scratch_shapes=[pltpu.VMEM((na, n_sel, PAGE_SIZE, HEAD_DIM), F32),
                        pltpu.VMEM((na, n_sel, PAGE_SIZE, HEAD_DIM), F32),
                        pltpu.SemaphoreType.DMA((2,))],
    )
    return pl.pallas_call(
        functools.partial(_moba_decode_kernel, layer=layer, n_sel=n_sel, scale=HEAD_DIM ** -0.5),
        grid_spec=grid_spec,
        out_shape=jax.ShapeDtypeStruct((s, na, HEAD_DIM), F32),
        compiler_params=_cparams("arbitrary"),
        name="moba_decode_attention",
    )(page_table, ids, q, k_new, v_new, cache_k, cache_v)


def _page_copy(kidx_hbm, buf, sem, layer, page, slot):
    return pltpu.make_async_copy(kidx_hbm.at[layer, page], buf.at[slot], sem)


def _dsa_decode_select_kernel(pt_ref, qi_ref, wi_ref, kin_ref, kidx_hbm, out_ref, buf, s_ref, d_ref, sem,
                              *, layer, n_pages, n_keep):
    b = pl.program_id(0)
    ps = PAGE_SIZE

    def start(p, _):
        _page_copy(kidx_hbm, buf, sem, layer, pt_ref[b, p], p).start()
        return 0

    lax.fori_loop(0, n_pages, start, 0)
    qi = qi_ref[0].astype(BF16)
    wi = wi_ref[0]

    def wait(p, _):
        _page_copy(kidx_hbm, buf, sem, layer, 0, p).wait()
        return 0

    lax.fori_loop(0, n_pages, wait, 0)

    def score(p, _):
        d = _dot_nt(qi, buf[p].astype(BF16))
        s_ref[pl.ds(p, 1), :] = jnp.sum(wi * jnp.maximum(d, 0.0), axis=0, keepdims=True)
        return 0

    lax.fori_loop(0, n_pages, score, 0, unroll=LOOP_UNROLL)
    d_new = jnp.sum(qi.astype(F32) * kin_ref[0].astype(BF16).astype(F32), axis=-1, keepdims=True)
    s_new = jnp.sum(wi * jnp.maximum(d_new, 0.0), axis=0, keepdims=True)

    sc = s_ref[...]
    row = lax.broadcasted_iota(jnp.int32, sc.shape, 0)
    col = lax.broadcasted_iota(jnp.int32, sc.shape, 1)
    idx = row * ps + col
    idx_new = n_pages * ps

    def total(a):
        return jnp.sum(jnp.sum(a, axis=1, keepdims=True), axis=0, keepdims=True)

    def count(pred_all, pred_new):
        return total(jnp.where(pred_all, 1.0, 0.0)) + jnp.where(pred_new, 1.0, 0.0)

    def bit_step(i, u):
        cand = u | jnp.left_shift(jnp.int32(1), 31 - i)
        thr = _key_to_f32(cand)
        return jnp.where(count(sc >= thr, s_new >= thr) >= n_keep, cand, u)

    thr = _key_to_f32(lax.fori_loop(0, 32, bit_step, jnp.zeros((1, 1), jnp.int32)))
    need = n_keep - count(sc > thr, s_new > thr)
    n_bits = int(idx_new).bit_length()

    def idx_step(i, x):
        cand = x | jnp.left_shift(jnp.int32(1), n_bits - 1 - i)
        below = count((sc == thr) & (idx < cand), (s_new == thr) & (idx_new < cand))
        return jnp.where(below < need, cand, x)

    cut = lax.fori_loop(0, n_bits, idx_step, jnp.zeros((1, 1), jnp.int32))
    sel = (sc > thr) | ((sc == thr) & (idx <= cut))
    sel_new = (s_new > thr) | ((s_new == thr) & (idx_new <= cut))

    selb = jnp.where(sel, 1.0, 0.0).astype(BF16)
    r2 = lax.broadcasted_iota(jnp.int32, (ps, ps), 0)
    c2 = lax.broadcasted_iota(jnp.int32, (ps, ps), 1)
    incl = _dot(selb, jnp.where(r2 <= c2, 1.0, 0.0).astype(BF16))
    rp = lax.broadcasted_iota(jnp.int32, (n_pages, n_pages), 0)
    cp = lax.broadcasted_iota(jnp.int32, (n_pages, n_pages), 1)
    page_off = _dot(jnp.where(cp < rp, 1.0, 0.0).astype(BF16), incl[:, ps - 1:ps].astype(BF16))
    d_ref[...] = jnp.where(sel, page_off + incl - 1.0, -1.0)

    slot = lax.broadcasted_iota(jnp.int32, (n_keep, ps), 0).astype(F32)
    lane = lax.broadcasted_iota(jnp.int32, (1, ps), 1)
    tok = jnp.where(c2 == 0, r2, jnp.where(c2 == 1, 1, 0)).astype(BF16)

    def gather(p, acc):
        onehot = jnp.where(d_ref[pl.ds(p, 1), :] == slot, 1.0, 0.0).astype(BF16)
        return acc + _dot(onehot, tok) * jnp.where(lane == 1, jnp.asarray(p, F32), 1.0)

    acc = lax.fori_loop(0, n_pages, gather, jnp.zeros((n_keep, ps), F32), unroll=LOOP_UNROLL)
    found = acc[:, 0:1] + ps * acc[:, 1:2]
    last = lax.broadcasted_iota(jnp.int32, (n_keep, 1), 0) == n_keep - 1
    found = jnp.where(last & sel_new, float(idx_new), found)
    out_ref[0] = found.astype(jnp.int32)


def dsa_decode_select(page_table, qidx, widx, kidx_new, cache_kidx, layer):
    s, n_pages = page_table.shape
    n_keep = min(DSA_TOPK, max((n_pages * PAGE_SIZE + 1) // 4, 1))
    assert n_pages * PAGE_SIZE + 1 >= n_keep
    g = qidx.shape[1]
    grid_spec = pltpu.PrefetchScalarGridSpec(
        num_scalar_prefetch=1,
        grid=(s,),
        in_specs=[pl.BlockSpec((1, g, IDX_DIM), lambda b, pt: (b, 0, 0)),
                  pl.BlockSpec((1, g, 1), lambda b, pt: (b, 0, 0)),
                  pl.BlockSpec((1, 1, IDX_DIM), lambda b, pt: (b, 0, 0)),
                  pl.BlockSpec(memory_space=pl.ANY)],
        out_specs=pl.BlockSpec((1, n_keep, 1), lambda b, pt: (b, 0, 0)),
        scratch_shapes=[pltpu.VMEM((n_pages, PAGE_SIZE, IDX_DIM), F32),
                        pltpu.VMEM((n_pages, PAGE_SIZE), F32), pltpu.VMEM((n_pages, PAGE_SIZE), F32),
                        pltpu.SemaphoreType.DMA(())],
    )
    return pl.pallas_call(
        functools.partial(_dsa_decode_select_kernel, layer=layer, n_pages=n_pages, n_keep=n_keep),
        grid_spec=grid_spec,
        out_shape=jax.ShapeDtypeStruct((s, n_keep, 1), jnp.int32),
        compiler_params=_cparams("arbitrary"),
        name="dsa_decode_select",
    )(page_table, qidx, widx, kidx_new, cache_kidx)


def _row_copies(b, r, idx_ref, pt_ref, new_hbm, cache_hbm, buf, sem, layer, past_len):
    i = idx_ref[b, r]
    ic = jnp.minimum(i, past_len - 1)
    page = pt_ref[b, ic // PAGE_SIZE]
    dst = buf.at[pl.ds(pl.multiple_of(r * N_HEADS, N_HEADS), N_HEADS)]
    from_cache = pltpu.make_async_copy(cache_hbm.at[layer, page, ic % PAGE_SIZE], dst, sem)
    from_new = pltpu.make_async_copy(new_hbm.at[b], dst, sem)
    return i < past_len, from_cache, from_new


def _dsa_decode_attn_kernel(pt_ref, idx_ref, q_ref, kn_hbm, vn_hbm, ck_hbm, cv_hbm, o_ref, kbuf, vbuf, sems,
                            *, layer, n_keep, past_len, scale):
    b = pl.program_id(0)

    def start(r, _):
        for new_hbm, cache_hbm, buf, sem in ((kn_hbm, ck_hbm, kbuf, sems.at[0]), (vn_hbm, cv_hbm, vbuf, sems.at[1])):
            in_cache, from_cache, from_new = _row_copies(b, r, idx_ref, pt_ref, new_hbm, cache_hbm, buf, sem,
                                                         layer, past_len)

            @pl.when(in_cache)
            def _():
                from_cache.start()

            @pl.when(jnp.logical_not(in_cache))
            def _():
                from_new.start()
        return 0

    lax.fori_loop(0, n_keep, start, 0)

    def wait(r, _):
        for new_hbm, buf, sem in ((kn_hbm, kbuf, sems.at[0]), (vn_hbm, vbuf, sems.at[1])):
            pltpu.make_async_copy(new_hbm.at[b], buf.at[pl.ds(pl.multiple_of(r * N_HEADS, N_HEADS), N_HEADS)],
                                  sem).wait()
        return 0

    lax.fori_loop(0, n_keep, wait, 0)
    heads = range(N_HEADS)
    q16 = (q_ref[0] * scale).astype(BF16)
    s = _head_scores(q16, lambda h: _head_rows(kbuf, h, n_keep), heads)
    m = jnp.max(s, axis=-1, keepdims=True)
    p = jnp.exp(s - m)
    l = jnp.sum(p, axis=-1, keepdims=True)
    o_ref[0] = _head_mix(p.astype(BF16), lambda h: _head_rows(vbuf, h, n_keep), heads) / l


def dsa_decode_attention(page_table, idx, q, k_new, v_new, cache_k, cache_v, layer):
    s, n_pages = page_table.shape
    n_keep = idx.shape[1]
    grid_spec = pltpu.PrefetchScalarGridSpec(
        num_scalar_prefetch=2,
        grid=(s,),
        in_specs=[pl.BlockSpec((1, N_HEADS, HEAD_DIM), lambda b, pt, ix: (b, 0, 0))]
        + [pl.BlockSpec(memory_space=pl.ANY)] * 4,
        out_specs=pl.BlockSpec((1, N_HEADS, HEAD_DIM), lambda b, pt, ix: (b, 0, 0)),
        scratch_shapes=[pltpu.VMEM((n_keep * N_HEADS, HEAD_DIM), F32),
                        pltpu.VMEM((n_keep * N_HEADS, HEAD_DIM), F32),
                        pltpu.SemaphoreType.DMA((2,))],
    )
    return pl.pallas_call(
        functools.partial(_dsa_decode_attn_kernel, layer=layer, n_keep=n_keep,
                          past_len=n_pages * PAGE_SIZE, scale=HEAD_DIM ** -0.5),
        grid_spec=grid_spec,
        out_shape=jax.ShapeDtypeStruct((s, N_HEADS, HEAD_DIM), F32),
        compiler_params=_cparams("arbitrary"),
        name="dsa_decode_attention",
    )(page_table, idx, q, k_new, v_new, cache_k, cache_v)


def _rope_tables(pos, half, reps):
    inv = ROPE_THETA ** (-jnp.arange(half, dtype=F32) / half)
    ang = pos.astype(F32)[:, None] * inv[None, :]
    cos, sin = jnp.cos(ang), jnp.sin(ang)
    return jnp.tile(jnp.concatenate([cos, cos], axis=1), (1, reps)), \
        jnp.tile(jnp.concatenate([-sin, sin], axis=1), (1, reps))


def _idx_tables(pos):
    cos, sin = _rope_tables(pos, IDX_DIM // 2, IDX_HEADS + 1)
    n = pos.shape[0]
    pad = TAIL_WIDTH - cos.shape[1]
    return jnp.concatenate([cos, jnp.ones((n, pad), F32)], axis=1), \
        jnp.concatenate([sin, jnp.zeros((n, pad), F32)], axis=1)


TAIL_COLS = IDX_HEADS * IDX_DIM + IDX_DIM + IDX_HEADS
TAIL_WIDTH = -(-TAIL_COLS // LANES) * LANES


def _prompt_layer(x, i, pos, b, l, wts, tabs):
    (attn_norm, ffn_norm, w_in_even, w_out_even, w_in_odd, w_out_odd, w_tail, w_gate, w_up, w_down) = wts
    cos, sin, cos_t, sin_t = tabs
    d = x.shape[1]
    mw = N_HEADS * HEAD_DIM
    tm, tn = 512, 1024
    g = attn_norm[i][None, :]
    even = i % 2 == 0
    w_in = w_in_even[i // 2] if even else w_in_odd[i // 2]
    w_out = w_out_even[i // 2] if even else w_out_odd[i // 2]
    rope_q = (N_HEADS_MOBA * HEAD_DIM // tn) if even else mw // tn
    common = dict(tm=tm, tn=tn, half=HEAD_DIM // 2)
    qs = HEAD_DIM ** -0.5
    wa = N_HEADS_MOBA * HEAD_DIM
    colscale = jnp.where((jnp.arange(mw) < wa) | (not even), qs * LOG2E, qs).astype(F32)[None, :]
    (q16,) = norm_proj(x, g, w_in, cos, sin, col0=0, ncols=mw, rope_tiles=rope_q, want_bf16=True,
                       colscale=colscale, **common)
    k_out = norm_proj(x, g, w_in, cos, sin, col0=mw, ncols=mw, rope_tiles=rope_q, want_f32=True,
                      want_bf16=True, want_kmean=even, **common)
    v32, v16 = norm_proj(x, g, w_in, cos, sin, col0=2 * mw, ncols=mw, rope_tiles=0, want_f32=True,
                         want_bf16=True, **common)
    k32, k16 = k_out[0], k_out[1]
    sh = (b, l, mw)
    if even:
        kmean = k_out[2].reshape(b, l // MOBA_BLOCK, mw)
        o = even_attention(q16.reshape(sh), k16.reshape(sh), v16.reshape(sh), kmean)
        kidx = None
    else:
        (tail,) = norm_proj(x, g, w_tail[i // 2], cos_t, sin_t, col0=0, ncols=TAIL_WIDTH, tm=tm,
                            tn=TAIL_WIDTH, rope_tiles=1, half=IDX_DIM // 2, want_f32=True)
        c1 = IDX_HEADS * IDX_DIM
        qidx = tail[:, :c1].reshape(b, l, IDX_HEADS, IDX_DIM).transpose(0, 2, 1, 3).astype(BF16)
        kidx = tail[:, c1:c1 + IDX_DIM].reshape(b, l, IDX_DIM)
        widx = tail[:, c1 + IDX_DIM:TAIL_COLS].reshape(b, l, IDX_HEADS)
        o = dsa_attention(q16.reshape(sh), k16.reshape(sh), v16.reshape(sh), qidx, kidx.astype(BF16), widx)
    x = matmul_res(o.reshape(b * l, mw), w_out, x, tm=tm, tn=tn)
    hmid = gate_up(x, ffn_norm[i][None, :], w_gate[i], w_up[i], tm=tm, tn=512)
    x = matmul_res(hmid, w_down[i], x, tm=tm, tn=512)
    return x, k32, v32, kidx


def _decode_layer(x, i, page_table, ck, cv, cache_kidx, wts, tabs):
    (attn_norm, ffn_norm, w_in_even, w_out_even, w_in_odd, w_out_odd, w_tail, w_gate, w_up, w_down) = wts
    cos, sin, cos_t, sin_t = tabs
    s = x.shape[0]
    mw = N_HEADS * HEAD_DIM
    tm, tn = s, 1024
    g = attn_norm[i][None, :]
    even = i % 2 == 0
    w_in = w_in_even[i // 2] if even else w_in_odd[i // 2]
    w_out = w_out_even[i // 2] if even else w_out_odd[i // 2]
    rope_q = (N_HEADS_MOBA * HEAD_DIM // tn) if even else mw // tn
    common = dict(tm=tm, tn=tn, half=HEAD_DIM // 2, want_f32=True)
    (q,) = norm_proj(x, g, w_in, cos, sin, col0=0, ncols=mw, rope_tiles=rope_q, **common)
    (k,) = norm_proj(x, g, w_in, cos, sin, col0=mw, ncols=mw, rope_tiles=rope_q, **common)
    (v,) = norm_proj(x, g, w_in, cos, sin, col0=2 * mw, ncols=mw, rope_tiles=0, **common)
    hsh = (s, N_HEADS, HEAD_DIM)
    q3, k3, v3 = q.reshape(hsh), k.reshape(hsh), v.reshape(hsh)
    if even:
        page_rows = ck.shape[:2] + (PAGE_SIZE * N_HEADS, HEAD_DIM)
        o_sb = sb_decode_attention(page_table, q3, ck.reshape(page_rows), cv.reshape(page_rows), i)
        ids = moba_decode_gate(page_table, q3, ck, i)
        ids = jnp.transpose(ids[:, :MOBA_TOPK, :N_HEADS_MOBA], (0, 2, 1))
        o_moba = moba_decode_attention(page_table, ids, q3, k3, v3, ck, cv, i)
        o = jnp.concatenate([o_moba, o_sb], axis=1)
        kidx = None
    else:
        (tail,) = norm_proj(x, g, w_tail[i // 2], cos_t, sin_t, col0=0, ncols=TAIL_WIDTH, tm=tm,
                            tn=TAIL_WIDTH, rope_tiles=1, half=IDX_DIM // 2, want_f32=True)
        c1 = IDX_HEADS * IDX_DIM
        qidx = tail[:, :c1].reshape(s, IDX_HEADS, IDX_DIM)
        kidx = tail[:, c1:c1 + IDX_DIM].reshape(s, 1, IDX_DIM)
        widx = tail[:, c1 + IDX_DIM:TAIL_COLS].reshape(s, IDX_HEADS, 1)
        idx = dsa_decode_select(page_table, qidx, widx, kidx, cache_kidx, i // 2)
        o = dsa_decode_attention(page_table, idx[:, :, 0], q3, k3, v3, ck, cv, i)
    x = matmul_res(o.reshape(s, mw).astype(BF16), w_out, x, tm=tm, tn=tn)
    hmid = gate_up(x, ffn_norm[i][None, :], w_gate[i], w_up[i], tm=tm, tn=512)
    x = matmul_res(hmid, w_down[i], x, tm=tm, tn=512)
    return x, k, v, kidx


def kernel(x_prompt, x_sample, cache_k, cache_v, cache_kidx, page_table, attn_norm, ffn_norm, final_norm,
           w_in_even, w_out_even, w_in_odd, w_out_odd, w_gate, w_up, w_down):
    b, l, d = x_prompt.shape
    depth = attn_norm.shape[0]
    mw = N_HEADS * HEAD_DIM
    w_tail = jnp.pad(w_in_odd[:, :, 3 * mw:], ((0, 0), (0, 0), (0, TAIL_WIDTH - TAIL_COLS))).astype(BF16)
    wts = (attn_norm, ffn_norm, w_in_even.astype(BF16), w_out_even.astype(BF16),
           w_in_odd[:, :, :3 * mw].astype(BF16), w_out_odd.astype(BF16), w_tail,
           w_gate.astype(BF16), w_up.astype(BF16), w_down.astype(BF16))
    pos = jnp.arange(l, dtype=jnp.int32)
    tabs = _rope_tables(pos, HEAD_DIM // 2, 1) + _idx_tables(pos)

    x = x_prompt.reshape(b * l, d)
    ks, vs, kis = [], [], []
    for i in range(depth):
        x, k32, v32, kidx = _prompt_layer(x, i, pos, b, l, wts, tabs)
        ks.append(k32.reshape(b, l, N_HEADS, HEAD_DIM))
        vs.append(v32.reshape(b, l, N_HEADS, HEAD_DIM))
        if kidx is not None:
            kis.append(kidx)
    y_prompt = rmsnorm(x, final_norm[None, :], tm=512).reshape(b, l, d)

    db = x_sample.shape[0]
    assert x_sample.shape[1] == 1
    past_len = page_table.shape[1] * PAGE_SIZE
    ck, cv = cache_k, cache_v
    pos_d = jnp.full((db,), past_len, jnp.int32)
    tabs_d = _rope_tables(pos_d, HEAD_DIM // 2, 1) + _idx_tables(pos_d)
    xd = x_sample.reshape(db, d)
    kd, vd, kid = [], [], []
    for i in range(depth):
        xd, k32, v32, kidx = _decode_layer(xd, i, page_table, ck, cv, cache_kidx, wts, tabs_d)
        kd.append(k32.reshape(db, 1, N_HEADS, HEAD_DIM))
        vd.append(v32.reshape(db, 1, N_HEADS, HEAD_DIM))
        if kidx is not None:
            kid.append(kidx)
    y_sample = rmsnorm(xd, final_norm[None, :], tm=db).reshape(db, 1, d)
    return (y_prompt, y_sample, jnp.stack(ks), jnp.stack(vs), jnp.stack(kis),
            jnp.stack(kd), jnp.stack(vd), jnp.stack(kid))
```

```python
import functools

import jax
import jax.numpy as jnp
import numpy as np
from jax import lax
from jax.experimental import pallas as pl
from jax.experimental.pallas import tpu as pltpu

F32 = jnp.float32
BF16 = jnp.bfloat16
NEG_INF = float("-inf")

HEAD_DIM = 128
N_HEADS = 16
N_HEADS_MOBA = 8
MOBA_BLOCK = 256
MOBA_TOPK = 3
IDX_HEADS = 8
IDX_DIM = 64
DSA_TOPK = 256
ROPE_THETA = 10000.0
RMS_EPS = 1e-6
PAGE_SIZE = 128

LOG2E = 1.4426950408889634
HEADS_PER_STEP = 4
LOOP_UNROLL = 8
MASK_BIAS = -1e30
SB_DEAD = -110.0
ATTN_CHUNK = 512
LANES = 128
VMEM_LIMIT = 56 * 1024 * 1024

NT_DIMS = (((1,), (1,)), ((), ()))


def _cparams(*sem):
    return pltpu.CompilerParams(dimension_semantics=sem, vmem_limit_bytes=VMEM_LIMIT)


def _dot(a, b):
    return jnp.dot(a, b, preferred_element_type=F32)


def _dot_nt(a, b):
    return lax.dot_general(a, b, NT_DIMS, preferred_element_type=F32)


def _split3(x):
    hi = x.astype(BF16)
    r1 = x - hi.astype(F32)
    mid = r1.astype(BF16)
    lo = (r1 - mid.astype(F32)).astype(BF16)
    return hi, mid, lo


def _split2(x):
    hi = x.astype(BF16)
    return hi, (x - hi.astype(F32)).astype(BF16)


def _rope_cols(y, cos_ref, sin_ref, half):
    tn = y.shape[1]
    wide = cos_ref.shape[1] == tn and tn != LANES
    outs = []
    lane = lax.broadcasted_iota(jnp.int32, (y.shape[0], LANES), 1)
    for s in range(tn // LANES):
        ys = y[:, s * LANES:(s + 1) * LANES]
        if wide:
            c = cos_ref[:, s * LANES:(s + 1) * LANES]
            sn = sin_ref[:, s * LANES:(s + 1) * LANES]
        else:
            c = cos_ref[...]
            sn = sin_ref[...]
        if 2 * half == LANES:
            partner = pltpu.roll(ys, half, 1)
        else:
            partner = jnp.where(lane % (2 * half) < half,
                                pltpu.roll(ys, LANES - half, 1), pltpu.roll(ys, half, 1))
        outs.append(ys * c + partner * sn)
    return jnp.concatenate(outs, axis=1) if len(outs) > 1 else outs[0]


def _norm_proj_kernel(x_ref, g_ref, w_ref, cos_ref, sin_ref, *refs, n_col_tiles, rope_tiles, half,
                      want_f32, want_bf16, want_kmean, has_colscale):
    refs = list(refs)
    hn_ref = refs.pop()
    cs_ref = refs.pop(0) if has_colscale else None
    o32_ref = refs.pop(0) if want_f32 else None
    o16_ref = refs.pop(0) if want_bf16 else None
    km_ref = refs.pop(0) if want_kmean else None
    j = pl.program_id(1)

    @pl.when(j == 0)
    def _():
        x = x_ref[...]
        ms = jnp.mean(x * x, axis=-1, keepdims=True)
        hn_ref[...] = (x * lax.rsqrt(ms + RMS_EPS) * g_ref[...]).astype(BF16)

    acc = _dot(hn_ref[...], w_ref[...])

    def emit(y):
        if has_colscale:
            y = y * cs_ref[...]
        if want_f32:
            o32_ref[...] = y
        if want_bf16:
            o16_ref[...] = y.astype(BF16)
        if want_kmean:
            for r in range(y.shape[0] // MOBA_BLOCK):
                km_ref[r] = jnp.mean(y[r * MOBA_BLOCK:(r + 1) * MOBA_BLOCK], axis=0, keepdims=True)

    if rope_tiles == 0:
        emit(acc)
    elif rope_tiles >= n_col_tiles:
        emit(_rope_cols(acc, cos_ref, sin_ref, half))
    else:
        @pl.when(j < rope_tiles)
        def _():
            emit(_rope_cols(acc, cos_ref, sin_ref, half))

        @pl.when(j >= rope_tiles)
        def _():
            emit(acc)


def norm_proj(x, g, w, cos, sin, *, layer, col0, ncols, tm, tn, rope_tiles, half,
              want_f32=False, want_bf16=False, want_kmean=False, colscale=None):
    t, d = x.shape
    assert t % tm == 0 and ncols % tn == 0 and col0 % tn == 0 and cos.shape[0] % tm == 0
    pos_blocks = cos.shape[0] // tm
    cb0 = col0 // tn
    n_col_tiles = ncols // tn
    out_shape, out_specs = [], []
    if want_f32:
        out_shape.append(jax.ShapeDtypeStruct((t, ncols), F32))
        out_specs.append(pl.BlockSpec((tm, tn), lambda i, j: (i, j)))
    if want_bf16:
        out_shape.append(jax.ShapeDtypeStruct((t, ncols), BF16))
        out_specs.append(pl.BlockSpec((tm, tn), lambda i, j: (i, j)))
    if want_kmean:
        assert tm % MOBA_BLOCK == 0
        out_shape.append(jax.ShapeDtypeStruct((t // MOBA_BLOCK, 1, ncols), F32))
        out_specs.append(pl.BlockSpec((tm // MOBA_BLOCK, 1, tn), lambda i, j: (i, 0, j)))
    kern = functools.partial(_norm_proj_kernel, n_col_tiles=n_col_tiles, rope_tiles=rope_tiles,
                             half=half, want_f32=want_f32, want_bf16=want_bf16, want_kmean=want_kmean,
                             has_colscale=colscale is not None)
    tw = cos.shape[1]
    extra_in = [] if colscale is None else [colscale]
    extra_specs = [] if colscale is None else [pl.BlockSpec((1, tn), lambda i, j: (0, j))]
    return pl.pallas_call(
        kern,
        grid=(t // tm, n_col_tiles),
        in_specs=[
            pl.BlockSpec((tm, d), lambda i, j: (i, 0)),
            pl.BlockSpec((1, d), lambda i, j: (0, 0)),
            pl.BlockSpec((None, d, tn), lambda i, j: (layer, 0, cb0 + j)),
            pl.BlockSpec((tm, tw), lambda i, j: (i % pos_blocks, 0)),
            pl.BlockSpec((tm, tw), lambda i, j: (i % pos_blocks, 0)),
        ] + extra_specs,
        out_specs=out_specs,
        out_shape=out_shape,
        scratch_shapes=[pltpu.VMEM((tm, d), BF16)],
        compiler_params=_cparams("parallel", "arbitrary"),
        name="norm_proj",
    )(x, g, w, cos, sin, *extra_in)


def _matmul_res_kernel(a_ref, w_ref, r_ref, o_ref):
    o_ref[...] = r_ref[...] + _dot(a_ref[...], w_ref[...])


def matmul_res(a, w, res, *, layer, tm, tn):
    t, k = a.shape
    n = w.shape[2]
    assert t % tm == 0 and n % tn == 0
    return pl.pallas_call(
        _matmul_res_kernel,
        grid=(t // tm, n // tn),
        in_specs=[
            pl.BlockSpec((tm, k), lambda i, j: (i, 0)),
            pl.BlockSpec((None, k, tn), lambda i, j: (layer, 0, j)),
            pl.BlockSpec((tm, tn), lambda i, j: (i, j)),
        ],
        out_specs=pl.BlockSpec((tm, tn), lambda i, j: (i, j)),
        out_shape=jax.ShapeDtypeStruct((t, n), F32),
        compiler_params=_cparams("parallel", "arbitrary"),
        name="matmul_res",
    )(a, w, res)


def _gate_up_kernel(x_ref, g_ref, wg_ref, wu_ref, o_ref, hn_ref):
    @pl.when(pl.program_id(1) == 0)
    def _():
        x = x_ref[...]
        ms = jnp.mean(x * x, axis=-1, keepdims=True)
        hn_ref[...] = (x * lax.rsqrt(ms + RMS_EPS) * g_ref[...]).astype(BF16)

    hn = hn_ref[...]
    a = _dot(hn, wg_ref[...])
    u = _dot(hn, wu_ref[...])
    o_ref[...] = (a / (1.0 + jnp.exp(-a)) * u).astype(BF16)


def gate_up(x, g, wg, wu, *, layer, tm, tn):
    t, d = x.shape
    f = wg.shape[2]
    assert t % tm == 0 and f % tn == 0
    return pl.pallas_call(
        _gate_up_kernel,
        grid=(t // tm, f // tn),
        in_specs=[
            pl.BlockSpec((tm, d), lambda i, j: (i, 0)),
            pl.BlockSpec((1, d), lambda i, j: (0, 0)),
            pl.BlockSpec((None, d, tn), lambda i, j: (layer, 0, j)),
            pl.BlockSpec((None, d, tn), lambda i, j: (layer, 0, j)),
        ],
        out_specs=pl.BlockSpec((tm, tn), lambda i, j: (i, j)),
        out_shape=jax.ShapeDtypeStruct((t, f), BF16),
        scratch_shapes=[pltpu.VMEM((tm, d), BF16)],
        compiler_params=_cparams("parallel", "arbitrary"),
        name="gate_up",
    )(x, g, wg, wu)


def _rmsnorm_kernel(x_ref, g_ref, o_ref):
    x = x_ref[...]
    ms = jnp.mean(x * x, axis=-1, keepdims=True)
    o_ref[...] = x * lax.rsqrt(ms + RMS_EPS) * g_ref[...]


def rmsnorm(x, g, *, tm):
    t, d = x.shape
    assert t % tm == 0
    return pl.pallas_call(
        _rmsnorm_kernel,
        grid=(t // tm,),
        in_specs=[pl.BlockSpec((tm, d), lambda i: (i, 0)), pl.BlockSpec((1, d), lambda i: (0, 0))],
        out_specs=pl.BlockSpec((tm, d), lambda i: (i, 0)),
        out_shape=jax.ShapeDtypeStruct((t, d), F32),
        compiler_params=_cparams("parallel"),
        name="rmsnorm",
    )(x, g)


def _fold_lanes(x, op):
    r = x[:, :LANES]
    for i in range(1, x.shape[1] // LANES):
        r = op(r, x[:, i * LANES:(i + 1) * LANES])
    return r


def _head_cols(g):
    return slice(g * HEAD_DIM, (g + 1) * HEAD_DIM)


def _masked_attention(n_heads, tq, prep, scores, v_ref, s_refs, n_chunks, ck):
    heads = range(n_heads)

    def row_max(k0, mvecs, last):
        ctx = prep(k0)
        out = []
        for g in heads:
            s = scores(g, k0, ctx, last)
            s_refs[g][:, pl.ds(k0, ck)] = s
            out.append(jnp.maximum(mvecs[g], _fold_lanes(s, jnp.maximum)))
        return tuple(out)

    mvecs = lax.fori_loop(0, n_chunks - 1, lambda c, mv: row_max(pl.multiple_of(c * ck, ck), mv, False),
                          tuple(jnp.full((tq, LANES), NEG_INF, F32) for _ in heads))
    mvecs = row_max(pl.multiple_of((n_chunks - 1) * ck, ck), mvecs, True)
    ms = [jnp.max(mv, axis=-1, keepdims=True) for mv in mvecs]

    def accumulate(c, carry):
        k0 = pl.multiple_of(c * ck, ck)
        out = []
        for g in heads:
            lvec, acc = carry[g]
            p = jnp.exp2(s_refs[g][:, pl.ds(k0, ck)] - ms[g])
            out.append((lvec + _fold_lanes(p, jnp.add),
                        acc + _dot(p.astype(BF16), v_ref[0, pl.ds(k0, ck), _head_cols(g)])))
        return tuple(out)

    init = tuple((jnp.zeros((tq, LANES), F32), jnp.zeros((tq, HEAD_DIM), F32)) for _ in heads)
    res = lax.fori_loop(0, n_chunks, accumulate, init)
    return [acc / jnp.sum(lvec, axis=-1, keepdims=True) for lvec, acc in res]


def _moba_select(q, km, qi):
    tq = q.shape[0]
    nb = km.shape[0]
    km = jnp.concatenate([km, jnp.zeros((LANES - nb, HEAD_DIM), F32)], axis=0)
    km_hi, km_mid, km_lo = _split3(km)
    gate = _dot_nt(q, km_hi) + _dot_nt(q, km_mid) + _dot_nt(q, km_lo)
    n_idx = lax.broadcasted_iota(jnp.int32, (tq, LANES), 1)
    g = jnp.where(n_idx < qi, gate, NEG_INF)
    sel = (n_idx == qi) | (n_idx >= nb)
    for _ in range(min(MOBA_TOPK, nb)):
        mx = jnp.max(g, axis=-1, keepdims=True)
        first = jnp.min(jnp.where(g == mx, n_idx, LANES), axis=-1, keepdims=True)
        hit = n_idx == first
        sel = sel | (hit & (mx > NEG_INF))
        g = jnp.where(hit, NEG_INF, g)
    return jnp.where(sel, 0.0, MASK_BIAS).astype(BF16)


def _moba_tile(q_ref, k_ref, v_ref, km_ref, o_ref, s_ref, qi):
    n = q_ref.shape[2] // HEAD_DIM
    tq = q_ref.shape[1]
    ck = min(ATTN_CHUNK, k_ref.shape[1])
    n_chunks = ((qi + 1) * MOBA_BLOCK + ck - 1) // ck
    qa = [jnp.concatenate([q_ref[0, :, _head_cols(g)],
                           _moba_select(q_ref[0, :, _head_cols(g)], km_ref[0, :, _head_cols(g)], qi)], axis=1)
          for g in range(n)]
    q_pos = qi * tq + lax.broadcasted_iota(jnp.int32, (tq, ck), 0)
    col = lax.broadcasted_iota(jnp.int32, (tq, ck), 1)

    def block_onehot(k0):
        blk = (k0 + lax.broadcasted_iota(jnp.int32, (ck, LANES), 0)) // MOBA_BLOCK
        return jnp.where(blk == lax.broadcasted_iota(jnp.int32, (ck, LANES), 1), 1.0, 0.0).astype(BF16)

    def scores(g, k0, onehot, last):
        s = _dot_nt(qa[g], jnp.concatenate([k_ref[0, pl.ds(k0, ck), _head_cols(g)], onehot], axis=1))
        return jnp.where(k0 + col > q_pos, NEG_INF, s) if last else s

    outs = _masked_attention(n, tq, block_onehot, scores, v_ref, [s_ref.at[g] for g in range(n)], n_chunks, ck)
    for g in range(n):
        o_ref[0, :, _head_cols(g)] = outs[g].astype(o_ref.dtype)


def _sb_block(q, k, v, tri2, carry, acc, before):
    z = _dot_nt(q, k)
    ls = jnp.minimum(z, 0.0) - jnp.log(1.0 + jnp.exp(-jnp.abs(z)))
    lk = ls - z
    if before is not None:
        lk = jnp.where(before, lk, 0.0)
    cs = _dot(jnp.concatenate(_split2(lk), axis=1), tri2)
    w = jnp.exp(ls + (carry + cs))
    if before is not None:
        w = jnp.where(before, w, 0.0)
    acc = acc + _dot(w.astype(BF16), v)
    carry = carry + cs[:, 0:1] + lk[:, 0:1]
    return carry, acc


def _sb_alive(carries):
    m = jnp.max(carries[0])
    for c in carries[1:]:
        m = jnp.maximum(m, jnp.max(c))
    return m > SB_DEAD


def _tri2(blk):
    row = lax.broadcasted_iota(jnp.int32, (2 * blk, blk), 0)
    col = lax.broadcasted_iota(jnp.int32, (2 * blk, blk), 1)
    return jnp.where(row % blk > col, 1.0, 0.0).astype(BF16)


def _sb_tile(q_ref, k_ref, v_ref, o_ref, qi):
    tq = q_ref.shape[1]
    blk = tq
    n = q_ref.shape[2] // HEAD_DIM
    heads = range(n)
    qs = [q_ref[0, :, _head_cols(g)] for g in heads]
    row = lax.broadcasted_iota(jnp.int32, (tq, blk), 0)
    col = lax.broadcasted_iota(jnp.int32, (tq, blk), 1)
    tri2 = _tri2(blk)

    def blocks(k0, state, before):
        return tuple(_sb_block(qs[g], k_ref[0, pl.ds(k0, blk), _head_cols(g)],
                               v_ref[0, pl.ds(k0, blk), _head_cols(g)], tri2, *state[g], before) for g in heads)

    init = tuple((jnp.zeros((tq, 1), F32), jnp.zeros((tq, HEAD_DIM), F32)) for _ in heads)
    state = blocks(pl.multiple_of(qi * blk, blk), init, col < row)

    def live(c):
        return (c[0] < qi) & _sb_alive([st[0] for st in c[1]])

    def step(c):
        jj, st = c
        return jj + 1, blocks(pl.multiple_of((qi - 1 - jj) * blk, blk), st, None)

    _, state = lax.while_loop(live, step, (jnp.int32(0), state))
    for g in heads:
        o_ref[0, :, _head_cols(g)] = state[g][1].astype(o_ref.dtype)


def _even_attn_kernel(q_ref, k_ref, v_ref, km_ref, o_ref, b_ref, *, n_moba_groups):
    hg = pl.program_id(1)
    qi = pl.program_id(2)

    @pl.when(hg < n_moba_groups)
    def _():
        _moba_tile(q_ref, k_ref, v_ref, km_ref, o_ref, b_ref, qi)

    @pl.when(hg >= n_moba_groups)
    def _():
        _sb_tile(q_ref, k_ref, v_ref, o_ref, qi)


def even_attention(q, k, v, kmean):
    b, l, w = q.shape
    nh = w // HEAD_DIM
    tq = MOBA_BLOCK
    gw = HEADS_PER_STEP * HEAD_DIM
    assert l % tq == 0 and nh == N_HEADS and N_HEADS_MOBA % HEADS_PER_STEP == 0
    nb = l // MOBA_BLOCK
    return pl.pallas_call(
        functools.partial(_even_attn_kernel, n_moba_groups=N_HEADS_MOBA // HEADS_PER_STEP),
        grid=(b, nh // HEADS_PER_STEP, l // tq),
        in_specs=[
            pl.BlockSpec((1, tq, gw), lambda bi, h, i: (bi, i, h)),
            pl.BlockSpec((1, l, gw), lambda bi, h, i: (bi, 0, h)),
            pl.BlockSpec((1, l, gw), lambda bi, h, i: (bi, 0, h)),
            pl.BlockSpec((1, nb, gw), lambda bi, h, i: (bi, 0, h)),
        ],
        out_specs=pl.BlockSpec((1, tq, gw), lambda bi, h, i: (bi, i, h)),
        out_shape=jax.ShapeDtypeStruct((b, l, w), BF16),
        scratch_shapes=[pltpu.VMEM((HEADS_PER_STEP, tq, l), F32)],
        compiler_params=_cparams("parallel", "parallel", "arbitrary"),
        name="even_attention",
    )(q, k, v, kmean)


def _key_to_f32(u):
    t = u ^ jnp.int32(-2 ** 31)
    bits = t ^ ((t >> 31) & jnp.int32(0x7FFFFFFF))
    return lax.bitcast_convert_type(bits, F32)


def _dsa_build_bias(qi_ref, ki_ref, wi_ref, st_ref, s_ref, t, n_keep):
    tq = s_ref.shape[0]
    ck = tq
    n_chunks = t + 1
    key = lax.broadcasted_iota(jnp.int32, (ck, tq), 0)
    qry = lax.broadcasted_iota(jnp.int32, (ck, tq), 1)
    wi = wi_ref[0]

    def score_chunk(c, _):
        k0 = pl.multiple_of(c * ck, ck)
        kc = ki_ref[0, pl.ds(k0, ck), :]
        sc = jnp.zeros((ck, tq), F32)
        for g in range(IDX_HEADS):
            sc = sc + wi[g:g + 1, :] * jnp.maximum(_dot_nt(kc, qi_ref[0, g]), 0.0)
        st_ref[pl.ds(k0, ck), :] = jnp.where((c < t) | (key <= qry), sc, NEG_INF)
        return 0

    lax.fori_loop(0, n_chunks, score_chunk, 0)

    def count(pred):
        def body(c, acc):
            k0 = pl.multiple_of(c * ck, ck)
            hit = jnp.where(pred(st_ref[pl.ds(k0, ck), :], c), 1.0, 0.0)
            return acc + jnp.sum(hit.reshape(ck // 8, 8, tq), axis=0)
        acc = lax.fori_loop(0, n_chunks, body, jnp.zeros((8, tq), F32))
        return jnp.sum(acc, axis=0, keepdims=True)

    def bit_step(i, u):
        cand = u | jnp.left_shift(jnp.int32(1), 31 - i)
        thr = _key_to_f32(cand)
        return jnp.where(count(lambda s, c: s >= thr) >= n_keep, cand, u)

    u = lax.fori_loop(0, 32, bit_step, jnp.zeros((1, tq), jnp.int32))
    thr = _key_to_f32(u)
    pos = t * tq + lax.broadcasted_iota(jnp.int32, (1, tq), 1)
    keep_all = pos < n_keep
    n_ge = count(lambda s, c: s >= thr)
    tied = (n_ge > n_keep) & jnp.logical_not(keep_all)
    no_cut = jnp.full((1, tq), 2 ** 30, jnp.int32)

    def tie_break():
        need = n_keep - count(lambda s, c: s > thr)
        n_bits = max(1, int((s_ref.shape[1] - 1)).bit_length())

        def idx_step(i, x):
            cand = x | jnp.left_shift(jnp.int32(1), n_bits - 1 - i)
            below = count(lambda s, c: (s == thr) & (c * ck + key < cand))
            return jnp.where(below < need, cand, x)

        x = lax.fori_loop(0, n_bits, idx_step, jnp.zeros((1, tq), jnp.int32))
        return jnp.where(tied, x, no_cut)

    any_tied = jnp.max(jnp.where(tied, 1, 0)) > 0
    cut = lax.cond(any_tied, tie_break, lambda: no_cut)

    def to_bias(c, _):
        k0 = pl.multiple_of(c * ck, ck)
        s = st_ref[pl.ds(k0, ck), :]
        kept = (s > thr) | ((s == thr) & (c * ck + key <= cut)) | keep_all
        s_ref[:, pl.ds(k0, ck)] = jnp.where(kept & (s > NEG_INF), 0.0, NEG_INF).T
        return 0

    lax.fori_loop(0, n_chunks, to_bias, 0)

    @pl.when(n_chunks * ck < s_ref.shape[1])
    def _():
        s_ref[:, pl.ds(pl.multiple_of(n_chunks * ck, ck), ck)] = jnp.full((tq, ck), NEG_INF, F32)


def _dsa_attn_kernel(q_ref, k_ref, v_ref, qi_ref, ki_ref, wi_ref, o_ref, st_ref, s_ref, hs_ref, *, n_keep):
    t = pl.program_id(1)
    h = pl.program_id(2)
    tq = q_ref.shape[1]

    @pl.when(h == 0)
    def _():
        _dsa_build_bias(qi_ref, ki_ref, wi_ref, st_ref, s_ref, t, n_keep)

    n = q_ref.shape[2] // HEAD_DIM
    qs = [q_ref[0, :, _head_cols(g)] for g in range(n)]
    ck = min(ATTN_CHUNK, k_ref.shape[1])
    n_chunks = ((t + 1) * tq + ck - 1) // ck
    outs = _masked_attention(
        n, tq, lambda k0: s_ref[:, pl.ds(k0, ck)],
        lambda g, k0, bias, last: _dot_nt(qs[g], k_ref[0, pl.ds(k0, ck), _head_cols(g)]) + bias,
        v_ref, [hs_ref.at[g] for g in range(n)], n_chunks, ck)
    for g in range(n):
        o_ref[0, :, _head_cols(g)] = outs[g].astype(o_ref.dtype)


def dsa_attention(q, k, v, qidx, kidx, widx):
    b, l, w = q.shape
    nh = w // HEAD_DIM
    tq = 256
    assert l % tq == 0
    n_keep = min(DSA_TOPK, max(l // 4, 1))
    g = qidx.shape[1]
    gw = HEADS_PER_STEP * HEAD_DIM
    assert nh % HEADS_PER_STEP == 0
    return pl.pallas_call(
        functools.partial(_dsa_attn_kernel, n_keep=n_keep),
        grid=(b, l // tq, nh // HEADS_PER_STEP),
        in_specs=[
            pl.BlockSpec((1, tq, gw), lambda bi, i, h: (bi, i, h)),
            pl.BlockSpec((1, l, gw), lambda bi, i, h: (bi, 0, h)),
            pl.BlockSpec((1, l, gw), lambda bi, i, h: (bi, 0, h)),
            pl.BlockSpec((1, g, tq, IDX_DIM), lambda bi, i, h: (bi, 0, i, 0)),
            pl.BlockSpec((1, l, IDX_DIM), lambda bi, i, h: (bi, 0, 0)),
            pl.BlockSpec((1, g, tq), lambda bi, i, h: (bi, 0, i)),
        ],
        out_specs=pl.BlockSpec((1, tq, gw), lambda bi, i, h: (bi, i, h)),
        out_shape=jax.ShapeDtypeStruct((b, l, w), BF16),
        scratch_shapes=[pltpu.VMEM((l, tq), F32), pltpu.VMEM((tq, l), F32),
                        pltpu.VMEM((HEADS_PER_STEP, tq, l), F32)],
        compiler_params=_cparams("parallel", "arbitrary", "arbitrary"),
        name="dsa_attention",
    )(q, k, v, qidx, kidx, widx)


def _only_row(x, h):
    row = lax.broadcasted_iota(jnp.int32, x.shape, 0)
    return jnp.where(row == h, x, jnp.zeros_like(x))


def _head_rows(ref2d, h, n):
    return ref2d[pl.ds(h, n, stride=N_HEADS), :]


def _head_scores(q16, head_slice, heads):
    out = None
    for i, h in enumerate(heads):
        part = _dot_nt(_only_row(q16, i), head_slice(h).astype(BF16))
        out = part if out is None else out + part
    return out


def _head_mix(p16, head_slice, heads):
    out = None
    for i, h in enumerate(heads):
        part = _dot(_only_row(p16, i), head_slice(h).astype(BF16))
        out = part if out is None else out + part
    return out


def _moba_gate_kernel(pt_ref, q_ref, *refs, pg, n_groups):
    k_refs = refs[:pg]
    ids_ref, km_ref = refs[pg:]
    p = pl.program_id(1)
    na = N_HEADS_MOBA
    pages_per_block = MOBA_BLOCK // PAGE_SIZE
    ksum = None
    for u in range(pg):
        part = jnp.sum(k_refs[u][0, 0], axis=0)
        ksum = part if ksum is None else ksum + part
        if (u + 1) % pages_per_block == 0:
            km_ref[p * (pg // pages_per_block) + u // pages_per_block] = ksum * (1.0 / MOBA_BLOCK)
            ksum = None

    @pl.when(p == n_groups - 1)
    def _():
        nb = km_ref.shape[0]
        gate = jnp.sum(km_ref[...] * q_ref[0, 0:na, :][None], axis=-1)
        n_idx = lax.broadcasted_iota(jnp.int32, gate.shape, 0)
        ids_ref[0] = jnp.zeros(ids_ref.shape[1:], jnp.int32)
        for r in range(MOBA_TOPK):
            mx = jnp.max(gate, axis=0, keepdims=True)
            first = jnp.min(jnp.where(gate == mx, n_idx, nb), axis=0, keepdims=True)
            ids_ref[0, r:r + 1, 0:na] = first
            gate = jnp.where(n_idx == first, NEG_INF, gate)


def moba_decode_gate(page_table, q, cache_k, layer, pg=8):
    s, n_pages = page_table.shape
    assert n_pages % pg == 0 and pg % (MOBA_BLOCK // PAGE_SIZE) == 0
    n_groups = n_pages // pg
    nb = n_pages * PAGE_SIZE // MOBA_BLOCK
    na = N_HEADS_MOBA
    assert nb >= MOBA_TOPK and na == 8

    def kmap(u):
        return lambda b, p, pt: (layer, pt[b, p * pg + u], 0, 0, 0)

    grid_spec = pltpu.PrefetchScalarGridSpec(
        num_scalar_prefetch=1,
        grid=(s, n_groups),
        in_specs=[pl.BlockSpec((1, N_HEADS, HEAD_DIM), lambda b, p, pt: (b, 0, 0))]
        + [pl.BlockSpec((1, 1, PAGE_SIZE, na, HEAD_DIM), kmap(u)) for u in range(pg)],
        out_specs=pl.BlockSpec((1, 8, LANES), lambda b, p, pt: (b, 0, 0)),
        scratch_shapes=[pltpu.VMEM((nb, na, HEAD_DIM), F32)],
    )
    return pl.pallas_call(
        functools.partial(_moba_gate_kernel, pg=pg, n_groups=n_groups),
        grid_spec=grid_spec,
        out_shape=jax.ShapeDtypeStruct((s, 8, LANES), jnp.int32),
        compiler_params=_cparams("parallel", "arbitrary"),
        name="moba_decode_gate",
    )(page_table, q, *([cache_k] * pg))


def _sb_page_copies(pt_ref, ck_hbm, cv_hbm, kbuf, vbuf, sems, layer, b, page_idx, slot):
    page = pt_ref[b, page_idx]
    return (pltpu.make_async_copy(ck_hbm.at[layer, page], kbuf.at[slot], sems.at[0, slot]),
            pltpu.make_async_copy(cv_hbm.at[layer, page], vbuf.at[slot], sems.at[1, slot]))


def _sb_decode_kernel(pt_ref, q_ref, ck_hbm, cv_hbm, o_ref, kbuf, vbuf, sems, *, layer, n_pages, scale):
    b = pl.program_id(0)
    na = N_HEADS_MOBA
    nsb = N_HEADS - na
    tri2 = _tri2(PAGE_SIZE)
    qsb = (q_ref[0, na:, :] * scale).astype(BF16)

    def copies(i, slot):
        return _sb_page_copies(pt_ref, ck_hbm, cv_hbm, kbuf, vbuf, sems, layer, b, n_pages - 1 - i, slot)

    for cp in copies(0, 0):
        cp.start()

    def live(c):
        return (c[0] < n_pages) & _sb_alive([c[1]])

    def step(c):
        i, carry, acc = c
        slot = i % 2
        for cp in copies(i, slot):
            cp.wait()

        @pl.when(i + 1 < n_pages)
        def _():
            for cp in copies(i + 1, 1 - slot):
                cp.start()

        k2, v2 = kbuf.at[slot], vbuf.at[slot]
        z = _head_scores(qsb, lambda h: _head_rows(k2, na + h, PAGE_SIZE), range(nsb))
        ls = jnp.minimum(z, 0.0) - jnp.log(1.0 + jnp.exp(-jnp.abs(z)))
        lk = ls - z
        cs = _dot(jnp.concatenate(_split2(lk), axis=1), tri2)
        w = jnp.exp(ls + (carry + cs))
        acc = acc + _head_mix(w.astype(BF16), lambda h: _head_rows(v2, na + h, PAGE_SIZE), range(nsb))
        return i + 1, carry + cs[:, 0:1] + lk[:, 0:1], acc

    n_done, _, acc = lax.while_loop(live, step, (jnp.int32(0), jnp.zeros((nsb, 1), F32),
                                                 jnp.zeros((nsb, HEAD_DIM), F32)))

    @pl.when(n_done < n_pages)
    def _():
        for cp in copies(n_done, n_done % 2):
            cp.wait()

    o_ref[0] = acc


def sb_decode_attention(page_table, q, cache_k, cache_v, layer):
    s, n_pages = page_table.shape
    nsb = N_HEADS - N_HEADS_MOBA
    rows = PAGE_SIZE * N_HEADS
    grid_spec = pltpu.PrefetchScalarGridSpec(
        num_scalar_prefetch=1,
        grid=(s,),
        in_specs=[pl.BlockSpec((1, N_HEADS, HEAD_DIM), lambda b, pt: (b, 0, 0)),
                  pl.BlockSpec(memory_space=pl.ANY), pl.BlockSpec(memory_space=pl.ANY)],
        out_specs=pl.BlockSpec((1, nsb, HEAD_DIM), lambda b, pt: (b, 0, 0)),
        scratch_shapes=[pltpu.VMEM((2, rows, HEAD_DIM), F32), pltpu.VMEM((2, rows, HEAD_DIM), F32),
                        pltpu.SemaphoreType.DMA((2, 2))],
    )
    return pl.pallas_call(
        functools.partial(_sb_decode_kernel, layer=layer, n_pages=n_pages, scale=HEAD_DIM ** -0.5),
        grid_spec=grid_spec,
        out_shape=jax.ShapeDtypeStruct((s, nsb, HEAD_DIM), F32),
        compiler_params=_cparams("arbitrary"),
        name="sb_decode_attention",
    )(page_table, q, cache_k, cache_v)


def _moba_page_copy(cache_hbm, buf, sem, layer, page, h, u):
    return pltpu.make_async_copy(cache_hbm.at[layer, page, :, h], buf.at[h, u], sem)


def _moba_decode_kernel(pt_ref, ids_ref, q_ref, kn_ref, vn_ref, ck_hbm, cv_hbm, o_ref, kbuf, vbuf, sems,
                        *, layer, n_sel, scale):
    b = pl.program_id(0)
    ppb = MOBA_BLOCK // PAGE_SIZE
    na = N_HEADS_MOBA
    for h in range(na):
        for u in range(n_sel):
            page = pt_ref[b, ids_ref[b, h, u // ppb] * ppb + u % ppb]
            _moba_page_copy(ck_hbm, kbuf, sems.at[0], layer, page, h, u).start()
            _moba_page_copy(cv_hbm, vbuf, sems.at[1], layer, page, h, u).start()
    for h in range(na):
        for u in range(n_sel):
            _moba_page_copy(ck_hbm, kbuf, sems.at[0], layer, 0, h, u).wait()
            _moba_page_copy(cv_hbm, vbuf, sems.at[1], layer, 0, h, u).wait()

    q = q_ref[0]
    qr = q.astype(BF16).astype(F32)
    knew = kn_ref[0].astype(BF16).astype(F32)
    vnew = vn_ref[0].astype(BF16).astype(F32)
    s_own_all = jnp.sum(qr * knew, axis=-1, keepdims=True) * scale
    for h in range(na):
        q8 = jnp.broadcast_to(q[h:h + 1], (8, HEAD_DIM)).astype(BF16)
        s_own = s_own_all[h:h + 1]
        scores = [_dot_nt(q8, kbuf[h, u].astype(BF16))[0:1] * scale for u in range(n_sel)]
        m = s_own
        for sc in scores:
            m = jnp.maximum(m, jnp.max(sc, axis=-1, keepdims=True))
        p_own = jnp.exp(s_own - m)
        l = p_own
        acc = p_own * vnew[h:h + 1]
        for u, sc in enumerate(scores):
            pu = jnp.exp(sc - m)
            l = l + jnp.sum(pu, axis=-1, keepdims=True)
            acc = acc + _dot(jnp.broadcast_to(pu, (8, PAGE_SIZE)).astype(BF16), vbuf[h, u].astype(BF16))[0:1]
        o_ref[0, h:h + 1, :] = acc / l


def moba_decode_attention(page_table, ids, q, k_new, v_new, cache_k, cache_v, layer):
    s = page_table.shape[0]
    ppb = MOBA_BLOCK // PAGE_SIZE
    n_sel = ids.shape[2] * ppb
    na = N_HEADS_MOBA
    hspec = pl.BlockSpec((1, N_HEADS, HEAD_DIM), lambda b, pt, ids_: (b, 0, 0))
    grid_spec = pltpu.PrefetchScalarGridSpec(
        num_scalar_prefetch=2,
        grid=(s,),
        in_specs=[hspec, hspec, hspec, pl.BlockSpec(memory_space=pl.ANY), pl.BlockSpec(memory_space=pl.ANY)],
        out_specs=pl.BlockSpec((1, na, HEAD_DIM), lambda b, pt, ids_: (b, 0, 0)),
        scratch_shapes=[pltpu.VMEM((na, n_sel, PAGE_SIZE, HEAD_DIM), F32),
                        pltpu.VMEM((na, n_sel, PAGE_SIZE, HEAD_DIM), F32),
                        pltpu.SemaphoreType.DMA((2,))],
    )
    return pl.pallas_call(
        functools.partial(_moba_decode_kernel, layer=layer, n_sel=n_sel, scale=HEAD_DIM ** -0.5),
        grid_spec=grid_spec,
        out_shape=jax.ShapeDtypeStruct((s, na, HEAD_DIM), F32),
        compiler_params=_cparams("arbitrary"),
        name="moba_decode_attention",
    )(page_table, ids, q, k_new, v_new, cache_k, cache_v)


def _page_copy(kidx_hbm, buf, sem, layer, page, slot):
    return pltpu.make_async_copy(kidx_hbm.at[layer, page], buf.at[slot], sem)


def _dsa_decode_select_kernel(pt_ref, qi_ref, wi_ref, kin_ref, kidx_hbm, out_ref, buf, s_ref, d_ref, sem,
                              *, layer, n_pages, n_keep):
    b = pl.program_id(0)
    ps = PAGE_SIZE

    def start(p, _):
        _page_copy(kidx_hbm, buf, sem, layer, pt_ref[b, p], p).start()
        return 0

    lax.fori_loop(0, n_pages, start, 0)
    qi = qi_ref[0].astype(BF16)
    wi = wi_ref[0]

    def wait(p, _):
        _page_copy(kidx_hbm, buf, sem, layer, 0, p).wait()
        return 0

    lax.fori_loop(0, n_pages, wait, 0)

    def score(p, _):
        d = _dot_nt(qi, buf[p].astype(BF16))
        s_ref[pl.ds(p, 1), :] = jnp.sum(wi * jnp.maximum(d, 0.0), axis=0, keepdims=True)
        return 0

    lax.fori_loop(0, n_pages, score, 0, unroll=LOOP_UNROLL)
    d_new = jnp.sum(qi.astype(F32) * kin_ref[0].astype(BF16).astype(F32), axis=-1, keepdims=True)
    s_new = jnp.sum(wi * jnp.maximum(d_new, 0.0), axis=0, keepdims=True)

    sc = s_ref[...]
    row = lax.broadcasted_iota(jnp.int32, sc.shape, 0)
    col = lax.broadcasted_iota(jnp.int32, sc.shape, 1)
    idx = row * ps + col
    idx_new = n_pages * ps

    def total(a):
        return jnp.sum(jnp.sum(a, axis=1, keepdims=True), axis=0, keepdims=True)

    def count(pred_all, pred_new):
        return total(jnp.where(pred_all, 1.0, 0.0)) + jnp.where(pred_new, 1.0, 0.0)

    def bit_step(i, u):
        cand = u | jnp.left_shift(jnp.int32(1), 31 - i)
        thr = _key_to_f32(cand)
        return jnp.where(count(sc >= thr, s_new >= thr) >= n_keep, cand, u)

    thr = _key_to_f32(lax.fori_loop(0, 32, bit_step, jnp.zeros((1, 1), jnp.int32)))
    need = n_keep - count(sc > thr, s_new > thr)
    n_bits = int(idx_new).bit_length()

    def idx_step(i, x):
        cand = x | jnp.left_shift(jnp.int32(1), n_bits - 1 - i)
        below = count((sc == thr) & (idx < cand), (s_new == thr) & (idx_new < cand))
        return jnp.where(below < need, cand, x)

    cut = lax.fori_loop(0, n_bits, idx_step, jnp.zeros((1, 1), jnp.int32))
    sel = (sc > thr) | ((sc == thr) & (idx <= cut))
    sel_new = (s_new > thr) | ((s_new == thr) & (idx_new <= cut))

    selb = jnp.where(sel, 1.0, 0.0).astype(BF16)
    r2 = lax.broadcasted_iota(jnp.int32, (ps, ps), 0)
    c2 = lax.broadcasted_iota(jnp.int32, (ps, ps), 1)
    incl = _dot(selb, jnp.where(r2 <= c2, 1.0, 0.0).astype(BF16))
    rp = lax.broadcasted_iota(jnp.int32, (n_pages, n_pages), 0)
    cp = lax.broadcasted_iota(jnp.int32, (n_pages, n_pages), 1)
    page_off = _dot(jnp.where(cp < rp, 1.0, 0.0).astype(BF16), incl[:, ps - 1:ps].astype(BF16))
    d_ref[...] = jnp.where(sel, page_off + incl - 1.0, -1.0)

    slot = lax.broadcasted_iota(jnp.int32, (n_keep, ps), 0).astype(F32)
    lane = lax.broadcasted_iota(jnp.int32, (1, ps), 1)
    tok = jnp.where(c2 == 0, r2, jnp.where(c2 == 1, 1, 0)).astype(BF16)

    def gather(p, acc):
        onehot = jnp.where(d_ref[pl.ds(p, 1), :] == slot, 1.0, 0.0).astype(BF16)
        return acc + _dot(onehot, tok) * jnp.where(lane == 1, jnp.asarray(p, F32), 1.0)

    acc = lax.fori_loop(0, n_pages, gather, jnp.zeros((n_keep, ps), F32), unroll=LOOP_UNROLL)
    found = acc[:, 0:1] + ps * acc[:, 1:2]
    last = lax.broadcasted_iota(jnp.int32, (n_keep, 1), 0) == n_keep - 1
    found = jnp.where(last & sel_new, float(idx_new), found)
    out_ref[0] = found.astype(jnp.int32)


def dsa_decode_select(page_table, qidx, widx, kidx_new, cache_kidx, layer):
    s, n_pages = page_table.shape
    n_keep = min(DSA_TOPK, max((n_pages * PAGE_SIZE + 1) // 4, 1))
    assert n_pages * PAGE_SIZE + 1 >= n_keep
    g = qidx.shape[1]
    grid_spec = pltpu.PrefetchScalarGridSpec(
        num_scalar_prefetch=1,
        grid=(s,),
        in_specs=[pl.BlockSpec((1, g, IDX_DIM), lambda b, pt: (b, 0, 0)),
                  pl.BlockSpec((1, g, 1), lambda b, pt: (b, 0, 0)),
                  pl.BlockSpec((1, 1, IDX_DIM), lambda b, pt: (b, 0, 0)),
                  pl.BlockSpec(memory_space=pl.ANY)],
        out_specs=pl.BlockSpec((1, n_keep, 1), lambda b, pt: (b, 0, 0)),
        scratch_shapes=[pltpu.VMEM((n_pages, PAGE_SIZE, IDX_DIM), F32),
                        pltpu.VMEM((n_pages, PAGE_SIZE), F32), pltpu.VMEM((n_pages, PAGE_SIZE), F32),
                        pltpu.SemaphoreType.DMA(())],
    )
    return pl.pallas_call(
        functools.partial(_dsa_decode_select_kernel, layer=layer, n_pages=n_pages, n_keep=n_keep),
        grid_spec=grid_spec,
        out_shape=jax.ShapeDtypeStruct((s, n_keep, 1), jnp.int32),
        compiler_params=_cparams("arbitrary"),
        name="dsa_decode_select",
    )(page_table, qidx, widx, kidx_new, cache_kidx)


def _row_copies(b, r, idx_ref, pt_ref, new_hbm, cache_hbm, buf, sem, layer, past_len):
    i = idx_ref[b, r]
    ic = jnp.minimum(i, past_len - 1)
    page = pt_ref[b, ic // PAGE_SIZE]
    dst = buf.at[pl.ds(pl.multiple_of(r * N_HEADS, N_HEADS), N_HEADS)]
    from_cache = pltpu.make_async_copy(cache_hbm.at[layer, page, ic % PAGE_SIZE], dst, sem)
    from_new = pltpu.make_async_copy(new_hbm.at[b], dst, sem)
    return i < past_len, from_cache, from_new


def _dsa_decode_attn_kernel(pt_ref, idx_ref, q_ref, kn_hbm, vn_hbm, ck_hbm, cv_hbm, o_ref, kbuf, vbuf, sems,
                            *, layer, n_keep, past_len, scale):
    b = pl.program_id(0)

    def start(r, _):
        for new_hbm, cache_hbm, buf, sem in ((kn_hbm, ck_hbm, kbuf, sems.at[0]), (vn_hbm, cv_hbm, vbuf, sems.at[1])):
            in_cache, from_cache, from_new = _row_copies(b, r, idx_ref, pt_ref, new_hbm, cache_hbm, buf, sem,
                                                         layer, past_len)

            @pl.when(in_cache)
            def _():
                from_cache.start()

            @pl.when(jnp.logical_not(in_cache))
            def _():
                from_new.start()
        return 0

    lax.fori_loop(0, n_keep, start, 0)

    def wait(r, _):
        for new_hbm, buf, sem in ((kn_hbm, kbuf, sems.at[0]), (vn_hbm, vbuf, sems.at[1])):
            pltpu.make_async_copy(new_hbm.at[b], buf.at[pl.ds(pl.multiple_of(r * N_HEADS, N_HEADS), N_HEADS)],
                                  sem).wait()
        return 0

    lax.fori_loop(0, n_keep, wait, 0)
    heads = range(N_HEADS)
    q16 = (q_ref[0] * scale).astype(BF16)
    s = _head_scores(q16, lambda h: _head_rows(kbuf, h, n_keep), heads)
    m = jnp.max(s, axis=-1, keepdims=True)
    p = jnp.exp(s - m)
    l = jnp.sum(p, axis=-1, keepdims=True)
    o_ref[0] = _head_mix(p.astype(BF16), lambda h: _head_rows(vbuf, h, n_keep), heads) / l


def dsa_decode_attention(page_table, idx, q, k_new, v_new, cache_k, cache_v, layer):
    s, n_pages = page_table.shape
    n_keep = idx.shape[1]
    grid_spec = pltpu.PrefetchScalarGridSpec(
        num_scalar_prefetch=2,
        grid=(s,),
        in_specs=[pl.BlockSpec((1, N_HEADS, HEAD_DIM), lambda b, pt, ix: (b, 0, 0))]
        + [pl.BlockSpec(memory_space=pl.ANY)] * 4,
        out_specs=pl.BlockSpec((1, N_HEADS, HEAD_DIM), lambda b, pt, ix: (b, 0, 0)),
        scratch_shapes=[pltpu.VMEM((n_keep * N_HEADS, HEAD_DIM), F32),
                        pltpu.VMEM((n_keep * N_HEADS, HEAD_DIM), F32),
                        pltpu.SemaphoreType.DMA((2,))],
    )
    return pl.pallas_call(
        functools.partial(_dsa_decode_attn_kernel, layer=layer, n_keep=n_keep,
                          past_len=n_pages * PAGE_SIZE, scale=HEAD_DIM ** -0.5),
        grid_spec=grid_spec,
        out_shape=jax.ShapeDtypeStruct((s, N_HEADS, HEAD_DIM), F32),
        compiler_params=_cparams("arbitrary"),
        name="dsa_decode_attention",
    )(page_table, idx, q, k_new, v_new, cache_k, cache_v)


def _rope_tables(pos, half, reps):
    inv = ROPE_THETA ** (-np.arange(half, dtype=np.float64) / half)
    ang = pos.astype(np.float64)[:, None] * inv[None, :]
    cos, sin = np.cos(ang).astype(np.float32), np.sin(ang).astype(np.float32)
    return np.tile(np.concatenate([cos, cos], axis=1), (1, reps)), \
        np.tile(np.concatenate([-sin, sin], axis=1), (1, reps))


def _idx_tables(pos):
    cos, sin = _rope_tables(pos, IDX_DIM // 2, IDX_HEADS + 1)
    n = pos.shape[0]
    pad = TAIL_WIDTH - cos.shape[1]
    return np.concatenate([cos, np.ones((n, pad), np.float32)], axis=1), \
        np.concatenate([sin, np.zeros((n, pad), np.float32)], axis=1)


TAIL_COLS = IDX_HEADS * IDX_DIM + IDX_DIM + IDX_HEADS
TAIL_WIDTH = -(-TAIL_COLS // LANES) * LANES


def _prompt_layer(x, i, pos, b, l, wts, tabs):
    (attn_norm, ffn_norm, w_in_even, w_out_even, w_in_odd, w_out_odd, w_tail, w_gate, w_up, w_down) = wts
    cos, sin, cos_t, sin_t = tabs
    d = x.shape[1]
    mw = N_HEADS * HEAD_DIM
    tm, tn = 512, 1024
    g = attn_norm[i][None, :]
    even = i % 2 == 0
    w_in = w_in_even if even else w_in_odd
    w_out = w_out_even if even else w_out_odd
    lw = i // 2
    rope_q = (N_HEADS_MOBA * HEAD_DIM // tn) if even else mw // tn
    common = dict(layer=lw, tm=tm, tn=tn, half=HEAD_DIM // 2)
    qs = HEAD_DIM ** -0.5
    wa = N_HEADS_MOBA * HEAD_DIM
    colscale = jnp.where((jnp.arange(mw) < wa) | (not even), qs * LOG2E, qs).astype(F32)[None, :]
    (q16,) = norm_proj(x, g, w_in, cos, sin, col0=0, ncols=mw, rope_tiles=rope_q, want_bf16=True,
                       colscale=colscale, **common)
    k_out = norm_proj(x, g, w_in, cos, sin, col0=mw, ncols=mw, rope_tiles=rope_q, want_f32=True,
                      want_bf16=True, want_kmean=even, **common)
    v32, v16 = norm_proj(x, g, w_in, cos, sin, col0=2 * mw, ncols=mw, rope_tiles=0, want_f32=True,
                         want_bf16=True, **common)
    k32, k16 = k_out[0], k_out[1]
    sh = (b, l, mw)
    if even:
        kmean = k_out[2].reshape(b, l // MOBA_BLOCK, mw)
        o = even_attention(q16.reshape(sh), k16.reshape(sh), v16.reshape(sh), kmean)
        kidx = None
    else:
        (tail,) = norm_proj(x, g, w_tail, cos_t, sin_t, layer=lw, col0=0, ncols=TAIL_WIDTH, tm=tm,
                            tn=TAIL_WIDTH, rope_tiles=1, half=IDX_DIM // 2, want_f32=True)
        c1 = IDX_HEADS * IDX_DIM
        qidx = tail[:, :c1].reshape(b, l, IDX_HEADS, IDX_DIM).transpose(0, 2, 1, 3).astype(BF16)
        kidx = tail[:, c1:c1 + IDX_DIM].reshape(b, l, IDX_DIM)
        widx = tail[:, c1 + IDX_DIM:TAIL_COLS].reshape(b, l, IDX_HEADS).transpose(0, 2, 1)
        o = dsa_attention(q16.reshape(sh), k16.reshape(sh), v16.reshape(sh), qidx, kidx.astype(BF16), widx)
    x = matmul_res(o.reshape(b * l, mw), w_out, x, layer=lw, tm=tm, tn=tn)
    hmid = gate_up(x, ffn_norm[i][None, :], w_gate, w_up, layer=i, tm=tm, tn=512)
    x = matmul_res(hmid, w_down, x, layer=i, tm=tm, tn=512)
    return x, k32, v32, kidx


def _decode_layer(x, i, page_table, ck, cv, cache_kidx, wts, tabs):
    (attn_norm, ffn_norm, w_in_even, w_out_even, w_in_odd, w_out_odd, w_tail, w_gate, w_up, w_down) = wts
    cos, sin, cos_t, sin_t = tabs
    s = x.shape[0]
    mw = N_HEADS * HEAD_DIM
    tm, tn = s, 1024
    g = attn_norm[i][None, :]
    even = i % 2 == 0
    w_in = w_in_even if even else w_in_odd
    w_out = w_out_even if even else w_out_odd
    lw = i // 2
    rope_q = (N_HEADS_MOBA * HEAD_DIM // tn) if even else mw // tn
    common = dict(layer=lw, tm=tm, tn=tn, half=HEAD_DIM // 2, want_f32=True)
    (q,) = norm_proj(x, g, w_in, cos, sin, col0=0, ncols=mw, rope_tiles=rope_q, **common)
    (k,) = norm_proj(x, g, w_in, cos, sin, col0=mw, ncols=mw, rope_tiles=rope_q, **common)
    (v,) = norm_proj(x, g, w_in, cos, sin, col0=2 * mw, ncols=mw, rope_tiles=0, **common)
    hsh = (s, N_HEADS, HEAD_DIM)
    q3, k3, v3 = q.reshape(hsh), k.reshape(hsh), v.reshape(hsh)
    if even:
        page_rows = ck.shape[:2] + (PAGE_SIZE * N_HEADS, HEAD_DIM)
        o_sb = sb_decode_attention(page_table, q3, ck.reshape(page_rows), cv.reshape(page_rows), i)
        ids = moba_decode_gate(page_table, q3, ck, i)
        ids = jnp.transpose(ids[:, :MOBA_TOPK, :N_HEADS_MOBA], (0, 2, 1))
        o_moba = moba_decode_attention(page_table, ids, q3, k3, v3, ck, cv, i)
        o = jnp.concatenate([o_moba, o_sb], axis=1)
        kidx = None
    else:
        (tail,) = norm_proj(x, g, w_tail, cos_t, sin_t, layer=lw, col0=0, ncols=TAIL_WIDTH, tm=tm,
                            tn=TAIL_WIDTH, rope_tiles=1, half=IDX_DIM // 2, want_f32=True)
        c1 = IDX_HEADS * IDX_DIM
        qidx = tail[:, :c1].reshape(s, IDX_HEADS, IDX_DIM)
        kidx = tail[:, c1:c1 + IDX_DIM].reshape(s, 1, IDX_DIM)
        widx = tail[:, c1 + IDX_DIM:TAIL_COLS].reshape(s, IDX_HEADS, 1)
        idx = dsa_decode_select(page_table, qidx, widx, kidx, cache_kidx, i // 2)
        o = dsa_decode_attention(page_table, idx[:, :, 0], q3, k3, v3, ck, cv, i)
    x = matmul_res(o.reshape(s, mw).astype(BF16), w_out, x, layer=lw, tm=tm, tn=tn)
    hmid = gate_up(x, ffn_norm[i][None, :], w_gate, w_up, layer=i, tm=tm, tn=512)
    x = matmul_res(hmid, w_down, x, layer=i, tm=tm, tn=512)
    return x, k, v, kidx


def kernel(x_prompt, x_sample, cache_k, cache_v, cache_kidx, page_table, attn_norm, ffn_norm, final_norm,
           w_in_even, w_out_even, w_in_odd, w_out_odd, w_gate, w_up, w_down):
    b, l, d = x_prompt.shape
    depth = attn_norm.shape[0]
    mw = N_HEADS * HEAD_DIM
    w_tail = jnp.pad(w_in_odd[:, :, 3 * mw:], ((0, 0), (0, 0), (0, TAIL_WIDTH - TAIL_COLS))).astype(BF16)
    wts = (attn_norm, ffn_norm, w_in_even.astype(BF16), w_out_even.astype(BF16),
           w_in_odd.astype(BF16), w_out_odd.astype(BF16), w_tail,
           w_gate.astype(BF16), w_up.astype(BF16), w_down.astype(BF16))
    pos = np.arange(l, dtype=np.int32)
    tabs = tuple(jnp.asarray(a) for a in _rope_tables(pos, HEAD_DIM // 2, 1) + _idx_tables(pos))

    x = x_prompt.reshape(b * l, d)
    ks, vs, kis = [], [], []
    for i in range(depth):
        x, k32, v32, kidx = _prompt_layer(x, i, pos, b, l, wts, tabs)
        ks.append(k32.reshape(b, l, N_HEADS, HEAD_DIM))
        vs.append(v32.reshape(b, l, N_HEADS, HEAD_DIM))
        if kidx is not None:
            kis.append(kidx)
    y_prompt = rmsnorm(x, final_norm[None, :], tm=512).reshape(b, l, d)

    db = x_sample.shape[0]
    assert x_sample.shape[1] == 1
    past_len = page_table.shape[1] * PAGE_SIZE
    ck, cv = cache_k, cache_v
    pos_d = np.full((db,), past_len, np.int32)
    tabs_d = tuple(jnp.asarray(a) for a in _rope_tables(pos_d, HEAD_DIM // 2, 1) + _idx_tables(pos_d))
    xd = x_sample.reshape(db, d)
    kd, vd, kid = [], [], []
    for i in range(depth):
        xd, k32, v32, kidx = _decode_layer(xd, i, page_table, ck, cv, cache_kidx, wts, tabs_d)
        kd.append(k32.reshape(db, 1, N_HEADS, HEAD_DIM))
        vd.append(v32.reshape(db, 1, N_HEADS, HEAD_DIM))
        if kidx is not None:
            kid.append(kidx)
    y_sample = rmsnorm(xd, final_norm[None, :], tm=db).reshape(db, 1, d)
    return (y_prompt, y_sample, jnp.stack(ks), jnp.stack(vs), jnp.stack(kis),
            jnp.stack(kd), jnp.stack(vd), jnp.stack(kid))
```

```python
import functools

import jax
import jax.numpy as jnp
import numpy as np
from jax import lax
from jax.experimental import pallas as pl
from jax.experimental.pallas import tpu as pltpu

F32 = jnp.float32
BF16 = jnp.bfloat16
NEG_INF = float("-inf")

HEAD_DIM = 128
N_HEADS = 16
N_HEADS_MOBA = 8
MOBA_BLOCK = 256
MOBA_TOPK = 3
IDX_HEADS = 8
IDX_DIM = 64
DSA_TOPK = 256
ROPE_THETA = 10000.0
RMS_EPS = 1e-6
PAGE_SIZE = 128

LOG2E = 1.4426950408889634
HEADS_PER_STEP = 4
LOOP_UNROLL = 8
MASK_BIAS = -1e30
SB_DEAD = -110.0
PROMPT_ROW_TILE = 1024
ATTN_CHUNK = 512
LANES = 128
VMEM_LIMIT = 56 * 1024 * 1024

NT_DIMS = (((1,), (1,)), ((), ()))


def _cparams(*sem):
    return pltpu.CompilerParams(dimension_semantics=sem, vmem_limit_bytes=VMEM_LIMIT)


def _dot(a, b):
    return jnp.dot(a, b, preferred_element_type=F32)


def _dot_nt(a, b):
    return lax.dot_general(a, b, NT_DIMS, preferred_element_type=F32)


def _split3(x):
    hi = x.astype(BF16)
    r1 = x - hi.astype(F32)
    mid = r1.astype(BF16)
    lo = (r1 - mid.astype(F32)).astype(BF16)
    return hi, mid, lo


def _split2(x):
    hi = x.astype(BF16)
    return hi, (x - hi.astype(F32)).astype(BF16)


def _rope_cols(y, cos_ref, sin_ref, half):
    tn = y.shape[1]
    wide = cos_ref.shape[1] == tn and tn != LANES
    outs = []
    lane = lax.broadcasted_iota(jnp.int32, (y.shape[0], LANES), 1)
    for s in range(tn // LANES):
        ys = y[:, s * LANES:(s + 1) * LANES]
        if wide:
            c = cos_ref[:, s * LANES:(s + 1) * LANES]
            sn = sin_ref[:, s * LANES:(s + 1) * LANES]
        else:
            c = cos_ref[...]
            sn = sin_ref[...]
        if 2 * half == LANES:
            partner = pltpu.roll(ys, half, 1)
        else:
            partner = jnp.where(lane % (2 * half) < half,
                                pltpu.roll(ys, LANES - half, 1), pltpu.roll(ys, half, 1))
        outs.append(ys * c + partner * sn)
    return jnp.concatenate(outs, axis=1) if len(outs) > 1 else outs[0]


def _norm_proj_kernel(x_ref, g_ref, w_ref, cos_ref, sin_ref, *refs, n_col_tiles, rope_tiles, half,
                      want_f32, want_bf16, want_kmean, has_colscale, side_cols):
    refs = list(refs)
    hn_ref = refs.pop()
    cs_ref = refs.pop(0) if has_colscale else None
    o32_ref = refs.pop(0) if want_f32 else None
    o16_ref = refs.pop(0) if want_bf16 else None
    km_ref = refs.pop(0) if want_kmean else None
    side_ref = refs.pop(0) if side_cols else None
    j = pl.program_id(1)

    @pl.when(j == 0)
    def _():
        x = x_ref[...]
        ms = jnp.mean(x * x, axis=-1, keepdims=True)
        hn_ref[...] = (x * lax.rsqrt(ms + RMS_EPS) * g_ref[...]).astype(BF16)

    acc = _dot(hn_ref[...], w_ref[...])

    def emit(y):
        if has_colscale:
            y = y * cs_ref[...]
        if want_f32:
            o32_ref[...] = y
        if want_bf16:
            o16_ref[...] = y.astype(BF16)
        if want_kmean:
            for r in range(y.shape[0] // MOBA_BLOCK):
                km_ref[r] = jnp.mean(y[r * MOBA_BLOCK:(r + 1) * MOBA_BLOCK], axis=0, keepdims=True)
        if side_cols:
            side_ref[...] = y[:, side_cols[0]:side_cols[0] + side_cols[1]]

    if rope_tiles == 0:
        emit(acc)
    elif rope_tiles >= n_col_tiles:
        emit(_rope_cols(acc, cos_ref, sin_ref, half))
    else:
        @pl.when(j < rope_tiles)
        def _():
            emit(_rope_cols(acc, cos_ref, sin_ref, half))

        @pl.when(j >= rope_tiles)
        def _():
            emit(acc)


def norm_proj(x, g, w, cos, sin, *, layer, col0, ncols, tm, tn, rope_tiles, half,
              want_f32=False, want_bf16=False, want_kmean=False, colscale=None, side_cols=None):
    t, d = x.shape
    assert t % tm == 0 and ncols % tn == 0 and col0 % tn == 0 and cos.shape[0] % tm == 0
    pos_blocks = cos.shape[0] // tm
    cb0 = col0 // tn
    n_col_tiles = ncols // tn
    out_shape, out_specs = [], []
    if want_f32:
        out_shape.append(jax.ShapeDtypeStruct((t, ncols), F32))
        out_specs.append(pl.BlockSpec((tm, tn), lambda i, j: (i, j)))
    if want_bf16:
        out_shape.append(jax.ShapeDtypeStruct((t, ncols), BF16))
        out_specs.append(pl.BlockSpec((tm, tn), lambda i, j: (i, j)))
    if want_kmean:
        assert tm % MOBA_BLOCK == 0
        out_shape.append(jax.ShapeDtypeStruct((t // MOBA_BLOCK, 1, ncols), F32))
        out_specs.append(pl.BlockSpec((tm // MOBA_BLOCK, 1, tn), lambda i, j: (i, 0, j)))
    if side_cols:
        assert n_col_tiles == 1
        out_shape.append(jax.ShapeDtypeStruct((t, side_cols[1]), F32))
        out_specs.append(pl.BlockSpec((tm, side_cols[1]), lambda i, j: (i, 0)))
    kern = functools.partial(_norm_proj_kernel, n_col_tiles=n_col_tiles, rope_tiles=rope_tiles,
                             half=half, want_f32=want_f32, want_bf16=want_bf16, want_kmean=want_kmean,
                             has_colscale=colscale is not None, side_cols=side_cols)
    tw = cos.shape[1]
    extra_in = [] if colscale is None else [colscale]
    extra_specs = [] if colscale is None else [pl.BlockSpec((1, tn), lambda i, j: (0, j))]
    return pl.pallas_call(
        kern,
        grid=(t // tm, n_col_tiles),
        in_specs=[
            pl.BlockSpec((tm, d), lambda i, j: (i, 0)),
            pl.BlockSpec((1, d), lambda i, j: (0, 0)),
            pl.BlockSpec((None, d, tn), lambda i, j: (layer, 0, cb0 + j)),
            pl.BlockSpec((tm, tw), lambda i, j: (i % pos_blocks, 0)),
            pl.BlockSpec((tm, tw), lambda i, j: (i % pos_blocks, 0)),
        ] + extra_specs,
        out_specs=out_specs,
        out_shape=out_shape,
        scratch_shapes=[pltpu.VMEM((tm, d), BF16)],
        compiler_params=_cparams("parallel", "arbitrary"),
        name="norm_proj",
    )(x, g, w, cos, sin, *extra_in)


def _matmul_res_kernel(a_ref, w_ref, r_ref, o_ref):
    o_ref[...] = r_ref[...] + _dot(a_ref[...], w_ref[...])


def matmul_res(a, w, res, *, layer, tm, tn):
    t, k = a.shape
    n = w.shape[2]
    assert t % tm == 0 and n % tn == 0
    return pl.pallas_call(
        _matmul_res_kernel,
        grid=(t // tm, n // tn),
        in_specs=[
            pl.BlockSpec((tm, k), lambda i, j: (i, 0)),
            pl.BlockSpec((None, k, tn), lambda i, j: (layer, 0, j)),
            pl.BlockSpec((tm, tn), lambda i, j: (i, j)),
        ],
        out_specs=pl.BlockSpec((tm, tn), lambda i, j: (i, j)),
        out_shape=jax.ShapeDtypeStruct((t, n), F32),
        compiler_params=_cparams("parallel", "arbitrary"),
        name="matmul_res",
    )(a, w, res)


def _gate_up_kernel(x_ref, g_ref, wg_ref, wu_ref, o_ref, hn_ref):
    @pl.when(pl.program_id(1) == 0)
    def _():
        x = x_ref[...]
        ms = jnp.mean(x * x, axis=-1, keepdims=True)
        hn_ref[...] = (x * lax.rsqrt(ms + RMS_EPS) * g_ref[...]).astype(BF16)

    hn = hn_ref[...]
    a = _dot(hn, wg_ref[...])
    u = _dot(hn, wu_ref[...])
    o_ref[...] = (a / (1.0 + jnp.exp(-a)) * u).astype(BF16)


def gate_up(x, g, wg, wu, *, layer, tm, tn):
    t, d = x.shape
    f = wg.shape[2]
    assert t % tm == 0 and f % tn == 0
    return pl.pallas_call(
        _gate_up_kernel,
        grid=(t // tm, f // tn),
        in_specs=[
            pl.BlockSpec((tm, d), lambda i, j: (i, 0)),
            pl.BlockSpec((1, d), lambda i, j: (0, 0)),
            pl.BlockSpec((None, d, tn), lambda i, j: (layer, 0, j)),
            pl.BlockSpec((None, d, tn), lambda i, j: (layer, 0, j)),
        ],
        out_specs=pl.BlockSpec((tm, tn), lambda i, j: (i, j)),
        out_shape=jax.ShapeDtypeStruct((t, f), BF16),
        scratch_shapes=[pltpu.VMEM((tm, d), BF16)],
        compiler_params=_cparams("parallel", "arbitrary"),
        name="gate_up",
    )(x, g, wg, wu)


def _rmsnorm_kernel(x_ref, g_ref, o_ref):
    x = x_ref[...]
    ms = jnp.mean(x * x, axis=-1, keepdims=True)
    o_ref[...] = x * lax.rsqrt(ms + RMS_EPS) * g_ref[...]


def rmsnorm(x, g, *, tm):
    t, d = x.shape
    assert t % tm == 0
    return pl.pallas_call(
        _rmsnorm_kernel,
        grid=(t // tm,),
        in_specs=[pl.BlockSpec((tm, d), lambda i: (i, 0)), pl.BlockSpec((1, d), lambda i: (0, 0))],
        out_specs=pl.BlockSpec((tm, d), lambda i: (i, 0)),
        out_shape=jax.ShapeDtypeStruct((t, d), F32),
        compiler_params=_cparams("parallel"),
        name="rmsnorm",
    )(x, g)


def _fold_lanes(x, op):
    r = x[:, :LANES]
    for i in range(1, x.shape[1] // LANES):
        r = op(r, x[:, i * LANES:(i + 1) * LANES])
    return r


def _head_cols(g):
    return slice(g * HEAD_DIM, (g + 1) * HEAD_DIM)


def _masked_attention(n_heads, tq, prep, scores, v_ref, s_refs, n_chunks, ck):
    heads = range(n_heads)

    def row_max(k0, mvecs, last):
        ctx = prep(k0)
        out = []
        for g in heads:
            s = scores(g, k0, ctx, last)
            s_refs[g][:, pl.ds(k0, ck)] = s
            out.append(jnp.maximum(mvecs[g], _fold_lanes(s, jnp.maximum)))
        return tuple(out)

    mvecs = lax.fori_loop(0, n_chunks - 1, lambda c, mv: row_max(pl.multiple_of(c * ck, ck), mv, False),
                          tuple(jnp.full((tq, LANES), NEG_INF, F32) for _ in heads))
    mvecs = row_max(pl.multiple_of((n_chunks - 1) * ck, ck), mvecs, True)
    ms = [jnp.max(mv, axis=-1, keepdims=True) for mv in mvecs]

    def accumulate(c, carry):
        k0 = pl.multiple_of(c * ck, ck)
        out = []
        for g in heads:
            lvec, acc = carry[g]
            p = jnp.exp2(s_refs[g][:, pl.ds(k0, ck)] - ms[g])
            out.append((lvec + _fold_lanes(p, jnp.add),
                        acc + _dot(p.astype(BF16), v_ref[0, pl.ds(k0, ck), _head_cols(g)])))
        return tuple(out)

    init = tuple((jnp.zeros((tq, LANES), F32), jnp.zeros((tq, HEAD_DIM), F32)) for _ in heads)
    res = lax.fori_loop(0, n_chunks, accumulate, init)
    return [acc / jnp.sum(lvec, axis=-1, keepdims=True) for lvec, acc in res]


def _moba_select(q, km, qi):
    tq = q.shape[0]
    nb = km.shape[0]
    km = jnp.concatenate([km, jnp.zeros((LANES - nb, HEAD_DIM), F32)], axis=0)
    km_hi, km_mid, km_lo = _split3(km)
    gate = _dot_nt(q, km_hi) + _dot_nt(q, km_mid) + _dot_nt(q, km_lo)
    n_idx = lax.broadcasted_iota(jnp.int32, (tq, LANES), 1)
    g = jnp.where(n_idx < qi, gate, NEG_INF)
    sel = (n_idx == qi) | (n_idx >= nb)
    for _ in range(min(MOBA_TOPK, nb)):
        mx = jnp.max(g, axis=-1, keepdims=True)
        first = jnp.min(jnp.where(g == mx, n_idx, LANES), axis=-1, keepdims=True)
        hit = n_idx == first
        sel = sel | (hit & (mx > NEG_INF))
        g = jnp.where(hit, NEG_INF, g)
    return jnp.where(sel, 0.0, MASK_BIAS).astype(BF16)


def _moba_tile(q_ref, k_ref, v_ref, km_ref, o_ref, s_ref, qi):
    n = q_ref.shape[2] // HEAD_DIM
    tq = q_ref.shape[1]
    ck = min(ATTN_CHUNK, k_ref.shape[1])
    n_chunks = ((qi + 1) * MOBA_BLOCK + ck - 1) // ck
    qa = [jnp.concatenate([q_ref[0, :, _head_cols(g)],
                           _moba_select(q_ref[0, :, _head_cols(g)], km_ref[0, :, _head_cols(g)], qi)], axis=1)
          for g in range(n)]
    q_pos = qi * tq + lax.broadcasted_iota(jnp.int32, (tq, ck), 0)
    col = lax.broadcasted_iota(jnp.int32, (tq, ck), 1)

    def block_onehot(k0):
        blk = (k0 + lax.broadcasted_iota(jnp.int32, (ck, LANES), 0)) // MOBA_BLOCK
        return jnp.where(blk == lax.broadcasted_iota(jnp.int32, (ck, LANES), 1), 1.0, 0.0).astype(BF16)

    def scores(g, k0, onehot, last):
        s = _dot_nt(qa[g], jnp.concatenate([k_ref[0, pl.ds(k0, ck), _head_cols(g)], onehot], axis=1))
        return jnp.where(k0 + col > q_pos, NEG_INF, s) if last else s

    outs = _masked_attention(n, tq, block_onehot, scores, v_ref, [s_ref.at[g] for g in range(n)], n_chunks, ck)
    for g in range(n):
        o_ref[0, :, _head_cols(g)] = outs[g].astype(o_ref.dtype)


def _sb_block(q, k, v, tri2, carry, acc, before):
    z = _dot_nt(q, k)
    ls = jnp.minimum(z, 0.0) - jnp.log(1.0 + jnp.exp(-jnp.abs(z)))
    lk = ls - z
    if before is not None:
        lk = jnp.where(before, lk, 0.0)
    cs = _dot(jnp.concatenate(_split2(lk), axis=1), tri2)
    w = jnp.exp(ls + (carry + cs))
    if before is not None:
        w = jnp.where(before, w, 0.0)
    acc = acc + _dot(w.astype(BF16), v)
    carry = carry + cs[:, 0:1] + lk[:, 0:1]
    return carry, acc


def _sb_alive(carries):
    m = jnp.max(carries[0])
    for c in carries[1:]:
        m = jnp.maximum(m, jnp.max(c))
    return m > SB_DEAD


def _tri2(blk):
    row = lax.broadcasted_iota(jnp.int32, (2 * blk, blk), 0)
    col = lax.broadcasted_iota(jnp.int32, (2 * blk, blk), 1)
    return jnp.where(row % blk > col, 1.0, 0.0).astype(BF16)


def _sb_tile(q_ref, k_ref, v_ref, o_ref, qi):
    tq = q_ref.shape[1]
    blk = tq
    n = q_ref.shape[2] // HEAD_DIM
    heads = range(n)
    qs = [q_ref[0, :, _head_cols(g)] for g in heads]
    row = lax.broadcasted_iota(jnp.int32, (tq, blk), 0)
    col = lax.broadcasted_iota(jnp.int32, (tq, blk), 1)
    tri2 = _tri2(blk)

    def blocks(k0, state, before):
        return tuple(_sb_block(qs[g], k_ref[0, pl.ds(k0, blk), _head_cols(g)],
                               v_ref[0, pl.ds(k0, blk), _head_cols(g)], tri2, *state[g], before) for g in heads)

    init = tuple((jnp.zeros((tq, 1), F32), jnp.zeros((tq, HEAD_DIM), F32)) for _ in heads)
    state = blocks(pl.multiple_of(qi * blk, blk), init, col < row)

    def live(c):
        return (c[0] < qi) & _sb_alive([st[0] for st in c[1]])

    def step(c):
        jj, st = c
        return jj + 1, blocks(pl.multiple_of((qi - 1 - jj) * blk, blk), st, None)

    _, state = lax.while_loop(live, step, (jnp.int32(0), state))
    for g in heads:
        o_ref[0, :, _head_cols(g)] = state[g][1].astype(o_ref.dtype)


def _even_attn_kernel(q_ref, k_ref, v_ref, km_ref, o_ref, b_ref, *, n_moba_groups):
    hg = pl.program_id(1)
    qi = pl.program_id(2)

    @pl.when(hg < n_moba_groups)
    def _():
        _moba_tile(q_ref, k_ref, v_ref, km_ref, o_ref, b_ref, qi)

    @pl.when(hg >= n_moba_groups)
    def _():
        _sb_tile(q_ref, k_ref, v_ref, o_ref, qi)


def even_attention(q, k, v, kmean):
    b, l, w = q.shape
    nh = w // HEAD_DIM
    tq = MOBA_BLOCK
    gw = HEADS_PER_STEP * HEAD_DIM
    assert l % tq == 0 and nh == N_HEADS and N_HEADS_MOBA % HEADS_PER_STEP == 0
    nb = l // MOBA_BLOCK
    return pl.pallas_call(
        functools.partial(_even_attn_kernel, n_moba_groups=N_HEADS_MOBA // HEADS_PER_STEP),
        grid=(b, nh // HEADS_PER_STEP, l // tq),
        in_specs=[
            pl.BlockSpec((1, tq, gw), lambda bi, h, i: (bi, i, h)),
            pl.BlockSpec((1, l, gw), lambda bi, h, i: (bi, 0, h)),
            pl.BlockSpec((1, l, gw), lambda bi, h, i: (bi, 0, h)),
            pl.BlockSpec((1, nb, gw), lambda bi, h, i: (bi, 0, h)),
        ],
        out_specs=pl.BlockSpec((1, tq, gw), lambda bi, h, i: (bi, i, h)),
        out_shape=jax.ShapeDtypeStruct((b, l, w), BF16),
        scratch_shapes=[pltpu.VMEM((HEADS_PER_STEP, tq, l), F32)],
        compiler_params=_cparams("parallel", "parallel", "arbitrary"),
        name="even_attention",
    )(q, k, v, kmean)


def _key_to_f32(u):
    t = u ^ jnp.int32(-2 ** 31)
    bits = t ^ ((t >> 31) & jnp.int32(0x7FFFFFFF))
    return lax.bitcast_convert_type(bits, F32)


def _dsa_build_bias(qi_ref, ki_ref, wi_ref, st_ref, s_ref, t, n_keep):
    tq = s_ref.shape[0]
    ck = tq
    n_chunks = t + 1
    key = lax.broadcasted_iota(jnp.int32, (ck, tq), 0)
    qry = lax.broadcasted_iota(jnp.int32, (ck, tq), 1)
    wi = wi_ref[0]

    def score_chunk(c, _):
        k0 = pl.multiple_of(c * ck, ck)
        kc = ki_ref[0, pl.ds(k0, ck), :]
        sc = jnp.zeros((ck, tq), F32)
        for g in range(IDX_HEADS):
            sc = sc + wi[g:g + 1, :] * jnp.maximum(_dot_nt(kc, qi_ref[0, g]), 0.0)
        st_ref[pl.ds(k0, ck), :] = jnp.where((c < t) | (key <= qry), sc, NEG_INF)
        return 0

    lax.fori_loop(0, n_chunks, score_chunk, 0)

    def count(pred):
        def body(c, acc):
            k0 = pl.multiple_of(c * ck, ck)
            hit = jnp.where(pred(st_ref[pl.ds(k0, ck), :], c), 1.0, 0.0)
            return acc + jnp.sum(hit.reshape(ck // 8, 8, tq), axis=0)
        acc = lax.fori_loop(0, n_chunks, body, jnp.zeros((8, tq), F32))
        return jnp.sum(acc, axis=0, keepdims=True)

    def bit_step(i, u):
        cand = u | jnp.left_shift(jnp.int32(1), 31 - i)
        thr = _key_to_f32(cand)
        return jnp.where(count(lambda s, c: s >= thr) >= n_keep, cand, u)

    u = lax.fori_loop(0, 32, bit_step, jnp.zeros((1, tq), jnp.int32))
    thr = _key_to_f32(u)
    pos = t * tq + lax.broadcasted_iota(jnp.int32, (1, tq), 1)
    keep_all = pos < n_keep
    n_ge = count(lambda s, c: s >= thr)
    tied = (n_ge > n_keep) & jnp.logical_not(keep_all)
    no_cut = jnp.full((1, tq), 2 ** 30, jnp.int32)

    def tie_break():
        need = n_keep - count(lambda s, c: s > thr)
        n_bits = max(1, int((s_ref.shape[1] - 1)).bit_length())

        def idx_step(i, x):
            cand = x | jnp.left_shift(jnp.int32(1), n_bits - 1 - i)
            below = count(lambda s, c: (s == thr) & (c * ck + key < cand))
            return jnp.where(below < need, cand, x)

        x = lax.fori_loop(0, n_bits, idx_step, jnp.zeros((1, tq), jnp.int32))
        return jnp.where(tied, x, no_cut)

    any_tied = jnp.max(jnp.where(tied, 1, 0)) > 0
    cut = lax.cond(any_tied, tie_break, lambda: no_cut)

    def to_bias(c, _):
        k0 = pl.multiple_of(c * ck, ck)
        s = st_ref[pl.ds(k0, ck), :]
        kept = (s > thr) | ((s == thr) & (c * ck + key <= cut)) | keep_all
        s_ref[:, pl.ds(k0, ck)] = jnp.where(kept & (s > NEG_INF), 0.0, NEG_INF).T
        return 0

    lax.fori_loop(0, n_chunks, to_bias, 0)

    @pl.when(n_chunks * ck < s_ref.shape[1])
    def _():
        s_ref[:, pl.ds(pl.multiple_of(n_chunks * ck, ck), ck)] = jnp.full((tq, ck), NEG_INF, F32)


def _dsa_attn_kernel(q_ref, k_ref, v_ref, qi_ref, ki_ref, wi_ref, o_ref, st_ref, s_ref, hs_ref, *, n_keep):
    t = pl.program_id(1)
    h = pl.program_id(2)
    tq = q_ref.shape[1]

    @pl.when(h == 0)
    def _():
        _dsa_build_bias(qi_ref, ki_ref, wi_ref, st_ref, s_ref, t, n_keep)

    n = q_ref.shape[2] // HEAD_DIM
    qs = [q_ref[0, :, _head_cols(g)] for g in range(n)]
    ck = min(ATTN_CHUNK, k_ref.shape[1])
    n_chunks = ((t + 1) * tq + ck - 1) // ck
    outs = _masked_attention(
        n, tq, lambda k0: s_ref[:, pl.ds(k0, ck)],
        lambda g, k0, bias, last: _dot_nt(qs[g], k_ref[0, pl.ds(k0, ck), _head_cols(g)]) + bias,
        v_ref, [hs_ref.at[g] for g in range(n)], n_chunks, ck)
    for g in range(n):
        o_ref[0, :, _head_cols(g)] = outs[g].astype(o_ref.dtype)


def dsa_attention(q, k, v, qidx, kidx, widx):
    b, l, w = q.shape
    nh = w // HEAD_DIM
    tq = 256
    assert l % tq == 0
    n_keep = min(DSA_TOPK, max(l // 4, 1))
    g = qidx.shape[1]
    gw = HEADS_PER_STEP * HEAD_DIM
    assert nh % HEADS_PER_STEP == 0
    return pl.pallas_call(
        functools.partial(_dsa_attn_kernel, n_keep=n_keep),
        grid=(b, l // tq, nh // HEADS_PER_STEP),
        in_specs=[
            pl.BlockSpec((1, tq, gw), lambda bi, i, h: (bi, i, h)),
            pl.BlockSpec((1, l, gw), lambda bi, i, h: (bi, 0, h)),
            pl.BlockSpec((1, l, gw), lambda bi, i, h: (bi, 0, h)),
            pl.BlockSpec((1, g, tq, IDX_DIM), lambda bi, i, h: (bi, 0, i, 0)),
            pl.BlockSpec((1, l, IDX_DIM), lambda bi, i, h: (bi, 0, 0)),
            pl.BlockSpec((1, g, tq), lambda bi, i, h: (bi, 0, i)),
        ],
        out_specs=pl.BlockSpec((1, tq, gw), lambda bi, i, h: (bi, i, h)),
        out_shape=jax.ShapeDtypeStruct((b, l, w), BF16),
        scratch_shapes=[pltpu.VMEM((l, tq), F32), pltpu.VMEM((tq, l), F32),
                        pltpu.VMEM((HEADS_PER_STEP, tq, l), F32)],
        compiler_params=_cparams("parallel", "arbitrary", "arbitrary"),
        name="dsa_attention",
    )(q, k, v, qidx, kidx, widx)


def _only_row(x, h):
    row = lax.broadcasted_iota(jnp.int32, x.shape, 0)
    return jnp.where(row == h, x, jnp.zeros_like(x))


def _head_rows(ref2d, h, n):
    return ref2d[pl.ds(h, n, stride=N_HEADS), :]


def _head_scores(q16, head_slice, heads):
    out = None
    for i, h in enumerate(heads):
        part = _dot_nt(_only_row(q16, i), head_slice(h).astype(BF16))
        out = part if out is None else out + part
    return out


def _head_mix(p16, head_slice, heads):
    out = None
    for i, h in enumerate(heads):
        part = _dot(_only_row(p16, i), head_slice(h).astype(BF16))
        out = part if out is None else out + part
    return out


def _moba_gate_kernel(pt_ref, q_ref, *refs, pg, n_groups):
    k_refs = refs[:pg]
    ids_ref, km_ref = refs[pg:]
    p = pl.program_id(1)
    na = N_HEADS_MOBA
    pages_per_block = MOBA_BLOCK // PAGE_SIZE
    ksum = None
    for u in range(pg):
        part = jnp.sum(k_refs[u][0, 0], axis=0)
        ksum = part if ksum is None else ksum + part
        if (u + 1) % pages_per_block == 0:
            km_ref[p * (pg // pages_per_block) + u // pages_per_block] = ksum * (1.0 / MOBA_BLOCK)
            ksum = None

    @pl.when(p == n_groups - 1)
    def _():
        nb = km_ref.shape[0]
        gate = jnp.sum(km_ref[...] * q_ref[0, 0:na, :][None], axis=-1)
        n_idx = lax.broadcasted_iota(jnp.int32, gate.shape, 0)
        ids_ref[0] = jnp.zeros(ids_ref.shape[1:], jnp.int32)
        for r in range(MOBA_TOPK):
            mx = jnp.max(gate, axis=0, keepdims=True)
            first = jnp.min(jnp.where(gate == mx, n_idx, nb), axis=0, keepdims=True)
            ids_ref[0, r:r + 1, 0:na] = first
            gate = jnp.where(n_idx == first, NEG_INF, gate)


def moba_decode_gate(page_table, q, cache_k, layer, pg=8):
    s, n_pages = page_table.shape
    assert n_pages % pg == 0 and pg % (MOBA_BLOCK // PAGE_SIZE) == 0
    n_groups = n_pages // pg
    nb = n_pages * PAGE_SIZE // MOBA_BLOCK
    na = N_HEADS_MOBA
    assert nb >= MOBA_TOPK and na == 8

    def kmap(u):
        return lambda b, p, pt: (layer, pt[b, p * pg + u], 0, 0, 0)

    grid_spec = pltpu.PrefetchScalarGridSpec(
        num_scalar_prefetch=1,
        grid=(s, n_groups),
        in_specs=[pl.BlockSpec((1, N_HEADS, HEAD_DIM), lambda b, p, pt: (b, 0, 0))]
        + [pl.BlockSpec((1, 1, PAGE_SIZE, na, HEAD_DIM), kmap(u)) for u in range(pg)],
        out_specs=pl.BlockSpec((1, 8, LANES), lambda b, p, pt: (b, 0, 0)),
        scratch_shapes=[pltpu.VMEM((nb, na, HEAD_DIM), F32)],
    )
    return pl.pallas_call(
        functools.partial(_moba_gate_kernel, pg=pg, n_groups=n_groups),
        grid_spec=grid_spec,
        out_shape=jax.ShapeDtypeStruct((s, 8, LANES), jnp.int32),
        compiler_params=_cparams("parallel", "arbitrary"),
        name="moba_decode_gate",
    )(page_table, q, *([cache_k] * pg))


def _sb_page_copies(pt_ref, ck_hbm, cv_hbm, kbuf, vbuf, sems, layer, b, page_idx, slot):
    page = pt_ref[b, page_idx]
    return (pltpu.make_async_copy(ck_hbm.at[layer, page], kbuf.at[slot], sems.at[0, slot]),
            pltpu.make_async_copy(cv_hbm.at[layer, page], vbuf.at[slot], sems.at[1, slot]))


def _sb_decode_kernel(pt_ref, q_ref, ck_hbm, cv_hbm, o_ref, kbuf, vbuf, sems, *, layer, n_pages, scale):
    b = pl.program_id(0)
    na = N_HEADS_MOBA
    nsb = N_HEADS - na
    tri2 = _tri2(PAGE_SIZE)
    qsb = (q_ref[0, na:, :] * scale).astype(BF16)

    def copies(i, slot):
        return _sb_page_copies(pt_ref, ck_hbm, cv_hbm, kbuf, vbuf, sems, layer, b, n_pages - 1 - i, slot)

    for cp in copies(0, 0):
        cp.start()

    def live(c):
        return (c[0] < n_pages) & _sb_alive([c[1]])

    def step(c):
        i, carry, acc = c
        slot = i % 2
        for cp in copies(i, slot):
            cp.wait()

        @pl.when(i + 1 < n_pages)
        def _():
            for cp in copies(i + 1, 1 - slot):
                cp.start()

        k2, v2 = kbuf.at[slot], vbuf.at[slot]
        z = _head_scores(qsb, lambda h: _head_rows(k2, na + h, PAGE_SIZE), range(nsb))
        ls = jnp.minimum(z, 0.0) - jnp.log(1.0 + jnp.exp(-jnp.abs(z)))
        lk = ls - z
        cs = _dot(jnp.concatenate(_split2(lk), axis=1), tri2)
        w = jnp.exp(ls + (carry + cs))
        acc = acc + _head_mix(w.astype(BF16), lambda h: _head_rows(v2, na + h, PAGE_SIZE), range(nsb))
        return i + 1, carry + cs[:, 0:1] + lk[:, 0:1], acc

    n_done, _, acc = lax.while_loop(live, step, (jnp.int32(0), jnp.zeros((nsb, 1), F32),
                                                 jnp.zeros((nsb, HEAD_DIM), F32)))

    @pl.when(n_done < n_pages)
    def _():
        for cp in copies(n_done, n_done % 2):
            cp.wait()

    o_ref[0] = acc


def sb_decode_attention(page_table, q, cache_k, cache_v, layer):
    s, n_pages = page_table.shape
    nsb = N_HEADS - N_HEADS_MOBA
    rows = PAGE_SIZE * N_HEADS
    grid_spec = pltpu.PrefetchScalarGridSpec(
        num_scalar_prefetch=1,
        grid=(s,),
        in_specs=[pl.BlockSpec((1, N_HEADS, HEAD_DIM), lambda b, pt: (b, 0, 0)),
                  pl.BlockSpec(memory_space=pl.ANY), pl.BlockSpec(memory_space=pl.ANY)],
        out_specs=pl.BlockSpec((1, nsb, HEAD_DIM), lambda b, pt: (b, 0, 0)),
        scratch_shapes=[pltpu.VMEM((2, rows, HEAD_DIM), F32), pltpu.VMEM((2, rows, HEAD_DIM), F32),
                        pltpu.SemaphoreType.DMA((2, 2))],
    )
    return pl.pallas_call(
        functools.partial(_sb_decode_kernel, layer=layer, n_pages=n_pages, scale=HEAD_DIM ** -0.5),
        grid_spec=grid_spec,
        out_shape=jax.ShapeDtypeStruct((s, nsb, HEAD_DIM), F32),
        compiler_params=_cparams("arbitrary"),
        name="sb_decode_attention",
    )(page_table, q, cache_k, cache_v)


def _moba_page_copy(cache_hbm, buf, sem, layer, page, h, u):
    return pltpu.make_async_copy(cache_hbm.at[layer, page, :, h], buf.at[h, u], sem)


def _moba_decode_kernel(pt_ref, ids_ref, q_ref, kn_ref, vn_ref, ck_hbm, cv_hbm, o_ref, kbuf, vbuf, sems,
                        *, layer, n_sel, scale):
    b = pl.program_id(0)
    ppb = MOBA_BLOCK // PAGE_SIZE
    na = N_HEADS_MOBA
    for h in range(na):
        for u in range(n_sel):
            page = pt_ref[b, ids_ref[b, h, u // ppb] * ppb + u % ppb]
            _moba_page_copy(ck_hbm, kbuf, sems.at[0], layer, page, h, u).start()
            _moba_page_copy(cv_hbm, vbuf, sems.at[1], layer, page, h, u).start()
    for h in range(na):
        for u in range(n_sel):
            _moba_page_copy(ck_hbm, kbuf, sems.at[0], layer, 0, h, u).wait()
            _moba_page_copy(cv_hbm, vbuf, sems.at[1], layer, 0, h, u).wait()

    q = q_ref[0]
    qr = q.astype(BF16).astype(F32)
    knew = kn_ref[0].astype(BF16).astype(F32)
    vnew = vn_ref[0].astype(BF16).astype(F32)
    s_own_all = jnp.sum(qr * knew, axis=-1, keepdims=True) * scale
    for h in range(na):
        q8 = jnp.broadcast_to(q[h:h + 1], (8, HEAD_DIM)).astype(BF16)
        s_own = s_own_all[h:h + 1]
        scores = [_dot_nt(q8, kbuf[h, u].astype(BF16))[0:1] * scale for u in range(n_sel)]
        m = s_own
        for sc in scores:
            m = jnp.maximum(m, jnp.max(sc, axis=-1, keepdims=True))
        p_own = jnp.exp(s_own - m)
        l = p_own
        acc = p_own * vnew[h:h + 1]
        for u, sc in enumerate(scores):
            pu = jnp.exp(sc - m)
            l = l + jnp.sum(pu, axis=-1, keepdims=True)
            acc = acc + _dot(jnp.broadcast_to(pu, (8, PAGE_SIZE)).astype(BF16), vbuf[h, u].astype(BF16))[0:1]
        o_ref[0, h:h + 1, :] = acc / l


def moba_decode_attention(page_table, ids, q, k_new, v_new, cache_k, cache_v, layer):
    s = page_table.shape[0]
    ppb = MOBA_BLOCK // PAGE_SIZE
    n_sel = ids.shape[2] * ppb
    na = N_HEADS_MOBA
    hspec = pl.BlockSpec((1, N_HEADS, HEAD_DIM), lambda b, pt, ids_: (b, 0, 0))
    grid_spec = pltpu.PrefetchScalarGridSpec(
        num_scalar_prefetch=2,
        grid=(s,),
        in_specs=[hspec, hspec, hspec, pl.BlockSpec(memory_space=pl.ANY), pl.BlockSpec(memory_space=pl.ANY)],
        out_specs=pl.BlockSpec((1, na, HEAD_DIM), lambda b, pt, ids_: (b, 0, 0)),
        scratch_shapes=[pltpu.VMEM((na, n_sel, PAGE_SIZE, HEAD_DIM), F32),
                        pltpu.VMEM((na, n_sel, PAGE_SIZE, HEAD_DIM), F32),
                        pltpu.SemaphoreType.DMA((2,))],
    )
    return pl.pallas_call(
        functools.partial(_moba_decode_kernel, layer=layer, n_sel=n_sel, scale=HEAD_DIM ** -0.5),
        grid_spec=grid_spec,
        out_shape=jax.ShapeDtypeStruct((s, na, HEAD_DIM), F32),
        compiler_params=_cparams("arbitrary"),
        name="moba_decode_attention",
    )(page_table, ids, q, k_new, v_new, cache_k, cache_v)


def _page_copy(kidx_hbm, buf, sem, layer, page, slot):
    return pltpu.make_async_copy(kidx_hbm.at[layer, page], buf.at[slot], sem)


def _dsa_decode_select_kernel(pt_ref, qi_ref, wi_ref, kin_ref, kidx_hbm, out_ref, buf, s_ref, d_ref, sem,
                              *, layer, n_pages, n_keep):
    b = pl.program_id(0)
    ps = PAGE_SIZE

    def start(p, _):
        _page_copy(kidx_hbm, buf, sem, layer, pt_ref[b, p], p).start()
        return 0

    lax.fori_loop(0, n_pages, start, 0)
    qi = qi_ref[0].astype(BF16)
    wi = wi_ref[0]

    def wait(p, _):
        _page_copy(kidx_hbm, buf, sem, layer, 0, p).wait()
        return 0

    lax.fori_loop(0, n_pages, wait, 0)

    def score(p, _):
        d = _dot_nt(qi, buf[p].astype(BF16))
        s_ref[pl.ds(p, 1), :] = jnp.sum(wi * jnp.maximum(d, 0.0), axis=0, keepdims=True)
        return 0

    lax.fori_loop(0, n_pages, score, 0, unroll=LOOP_UNROLL)
    d_new = jnp.sum(qi.astype(F32) * kin_ref[0].astype(BF16).astype(F32), axis=-1, keepdims=True)
    s_new = jnp.sum(wi * jnp.maximum(d_new, 0.0), axis=0, keepdims=True)

    sc = s_ref[...]
    row = lax.broadcasted_iota(jnp.int32, sc.shape, 0)
    col = lax.broadcasted_iota(jnp.int32, sc.shape, 1)
    idx = row * ps + col
    idx_new = n_pages * ps

    def total(a):
        return jnp.sum(jnp.sum(a, axis=1, keepdims=True), axis=0, keepdims=True)

    def count(pred_all, pred_new):
        return total(jnp.where(pred_all, 1.0, 0.0)) + jnp.where(pred_new, 1.0, 0.0)

    def bit_step(i, u):
        cand = u | jnp.left_shift(jnp.int32(1), 31 - i)
        thr = _key_to_f32(cand)
        return jnp.where(count(sc >= thr, s_new >= thr) >= n_keep, cand, u)

    thr = _key_to_f32(lax.fori_loop(0, 32, bit_step, jnp.zeros((1, 1), jnp.int32)))
    need = n_keep - count(sc > thr, s_new > thr)
    n_bits = int(idx_new).bit_length()

    def idx_step(i, x):
        cand = x | jnp.left_shift(jnp.int32(1), n_bits - 1 - i)
        below = count((sc == thr) & (idx < cand), (s_new == thr) & (idx_new < cand))
        return jnp.where(below < need, cand, x)

    cut = lax.fori_loop(0, n_bits, idx_step, jnp.zeros((1, 1), jnp.int32))
    sel = (sc > thr) | ((sc == thr) & (idx <= cut))
    sel_new = (s_new > thr) | ((s_new == thr) & (idx_new <= cut))

    selb = jnp.where(sel, 1.0, 0.0).astype(BF16)
    r2 = lax.broadcasted_iota(jnp.int32, (ps, ps), 0)
    c2 = lax.broadcasted_iota(jnp.int32, (ps, ps), 1)
    incl = _dot(selb, jnp.where(r2 <= c2, 1.0, 0.0).astype(BF16))
    rp = lax.broadcasted_iota(jnp.int32, (n_pages, n_pages), 0)
    cp = lax.broadcasted_iota(jnp.int32, (n_pages, n_pages), 1)
    page_off = _dot(jnp.where(cp < rp, 1.0, 0.0).astype(BF16), incl[:, ps - 1:ps].astype(BF16))
    d_ref[...] = jnp.where(sel, page_off + incl - 1.0, -1.0)

    slot = lax.broadcasted_iota(jnp.int32, (n_keep, ps), 0).astype(F32)
    lane = lax.broadcasted_iota(jnp.int32, (1, ps), 1)
    tok = jnp.where(c2 == 0, r2, jnp.where(c2 == 1, 1, 0)).astype(BF16)

    def gather(p, acc):
        onehot = jnp.where(d_ref[pl.ds(p, 1), :] == slot, 1.0, 0.0).astype(BF16)
        return acc + _dot(onehot, tok) * jnp.where(lane == 1, jnp.asarray(p, F32), 1.0)

    acc = lax.fori_loop(0, n_pages, gather, jnp.zeros((n_keep, ps), F32), unroll=LOOP_UNROLL)
    found = acc[:, 0:1] + ps * acc[:, 1:2]
    last = lax.broadcasted_iota(jnp.int32, (n_keep, 1), 0) == n_keep - 1
    found = jnp.where(last & sel_new, float(idx_new), found)
    out_ref[0] = found.astype(jnp.int32)


def dsa_decode_select(page_table, qidx, widx, kidx_new, cache_kidx, layer):
    s, n_pages = page_table.shape
    n_keep = min(DSA_TOPK, max((n_pages * PAGE_SIZE + 1) // 4, 1))
    assert n_pages * PAGE_SIZE + 1 >= n_keep
    g = qidx.shape[1]
    grid_spec = pltpu.PrefetchScalarGridSpec(
        num_scalar_prefetch=1,
        grid=(s,),
        in_specs=[pl.BlockSpec((1, g, IDX_DIM), lambda b, pt: (b, 0, 0)),
                  pl.BlockSpec((1, g, 1), lambda b, pt: (b, 0, 0)),
                  pl.BlockSpec((1, 1, IDX_DIM), lambda b, pt: (b, 0, 0)),
                  pl.BlockSpec(memory_space=pl.ANY)],
        out_specs=pl.BlockSpec((1, n_keep, 1), lambda b, pt: (b, 0, 0)),
        scratch_shapes=[pltpu.VMEM((n_pages, PAGE_SIZE, IDX_DIM), F32),
                        pltpu.VMEM((n_pages, PAGE_SIZE), F32), pltpu.VMEM((n_pages, PAGE_SIZE), F32),
                        pltpu.SemaphoreType.DMA(())],
    )
    return pl.pallas_call(
        functools.partial(_dsa_decode_select_kernel, layer=layer, n_pages=n_pages, n_keep=n_keep),
        grid_spec=grid_spec,
        out_shape=jax.ShapeDtypeStruct((s, n_keep, 1), jnp.int32),
        compiler_params=_cparams("arbitrary"),
        name="dsa_decode_select",
    )(page_table, qidx, widx, kidx_new, cache_kidx)


def _row_copies(b, r, idx_ref, pt_ref, new_hbm, cache_hbm, buf, sem, layer, past_len):
    i = idx_ref[b, r]
    ic = jnp.minimum(i, past_len - 1)
    page = pt_ref[b, ic // PAGE_SIZE]
    dst = buf.at[pl.ds(pl.multiple_of(r * N_HEADS, N_HEADS), N_HEADS)]
    from_cache = pltpu.make_async_copy(cache_hbm.at[layer, page, ic % PAGE_SIZE], dst, sem)
    from_new = pltpu.make_async_copy(new_hbm.at[b], dst, sem)
    return i < past_len, from_cache, from_new


def _dsa_decode_attn_kernel(pt_ref, idx_ref, q_ref, kn_hbm, vn_hbm, ck_hbm, cv_hbm, o_ref, kbuf, vbuf, sems,
                            *, layer, n_keep, past_len, scale):
    b = pl.program_id(0)

    def start(r, _):
        for new_hbm, cache_hbm, buf, sem in ((kn_hbm, ck_hbm, kbuf, sems.at[0]), (vn_hbm, cv_hbm, vbuf, sems.at[1])):
            in_cache, from_cache, from_new = _row_copies(b, r, idx_ref, pt_ref, new_hbm, cache_hbm, buf, sem,
                                                         layer, past_len)

            @pl.when(in_cache)
            def _():
                from_cache.start()

            @pl.when(jnp.logical_not(in_cache))
            def _():
                from_new.start()
        return 0

    lax.fori_loop(0, n_keep, start, 0)

    def wait(r, _):
        for new_hbm, buf, sem in ((kn_hbm, kbuf, sems.at[0]), (vn_hbm, vbuf, sems.at[1])):
            pltpu.make_async_copy(new_hbm.at[b], buf.at[pl.ds(pl.multiple_of(r * N_HEADS, N_HEADS), N_HEADS)],
                                  sem).wait()
        return 0

    lax.fori_loop(0, n_keep, wait, 0)
    heads = range(N_HEADS)
    q16 = (q_ref[0] * scale).astype(BF16)
    s = _head_scores(q16, lambda h: _head_rows(kbuf, h, n_keep), heads)
    m = jnp.max(s, axis=-1, keepdims=True)
    p = jnp.exp(s - m)
    l = jnp.sum(p, axis=-1, keepdims=True)
    o_ref[0] = _head_mix(p.astype(BF16), lambda h: _head_rows(vbuf, h, n_keep), heads) / l


def dsa_decode_attention(page_table, idx, q, k_new, v_new, cache_k, cache_v, layer):
    s, n_pages = page_table.shape
    n_keep = idx.shape[1]
    grid_spec = pltpu.PrefetchScalarGridSpec(
        num_scalar_prefetch=2,
        grid=(s,),
        in_specs=[pl.BlockSpec((1, N_HEADS, HEAD_DIM), lambda b, pt, ix: (b, 0, 0))]
        + [pl.BlockSpec(memory_space=pl.ANY)] * 4,
        out_specs=pl.BlockSpec((1, N_HEADS, HEAD_DIM), lambda b, pt, ix: (b, 0, 0)),
        scratch_shapes=[pltpu.VMEM((n_keep * N_HEADS, HEAD_DIM), F32),
                        pltpu.VMEM((n_keep * N_HEADS, HEAD_DIM), F32),
                        pltpu.SemaphoreType.DMA((2,))],
    )
    return pl.pallas_call(
        functools.partial(_dsa_decode_attn_kernel, layer=layer, n_keep=n_keep,
                          past_len=n_pages * PAGE_SIZE, scale=HEAD_DIM ** -0.5),
        grid_spec=grid_spec,
        out_shape=jax.ShapeDtypeStruct((s, N_HEADS, HEAD_DIM), F32),
        compiler_params=_cparams("arbitrary"),
        name="dsa_decode_attention",
    )(page_table, idx, q, k_new, v_new, cache_k, cache_v)


def _rope_tables(pos, half, reps):
    inv = ROPE_THETA ** (-np.arange(half, dtype=np.float64) / half)
    ang = pos.astype(np.float64)[:, None] * inv[None, :]
    cos, sin = np.cos(ang).astype(np.float32), np.sin(ang).astype(np.float32)
    return np.tile(np.concatenate([cos, cos], axis=1), (1, reps)), \
        np.tile(np.concatenate([-sin, sin], axis=1), (1, reps))


def _idx_tables(pos):
    cos, sin = _rope_tables(pos, IDX_DIM // 2, IDX_HEADS + 1)
    n = pos.shape[0]
    pad = TAIL_WIDTH - cos.shape[1]
    return np.concatenate([cos, np.ones((n, pad), np.float32)], axis=1), \
        np.concatenate([sin, np.zeros((n, pad), np.float32)], axis=1)


TAIL_COLS = IDX_HEADS * IDX_DIM + IDX_DIM + IDX_HEADS
TAIL_WIDTH = -(-TAIL_COLS // LANES) * LANES


def _prompt_layer(x, i, pos, b, l, wts, tabs):
    (attn_norm, ffn_norm, w_in_even, w_out_even, w_in_odd, w_out_odd, w_tail, w_gate, w_up, w_down) = wts
    cos, sin, cos_t, sin_t = tabs
    d = x.shape[1]
    mw = N_HEADS * HEAD_DIM
    tm, tn = PROMPT_ROW_TILE, 1024
    g = attn_norm[i][None, :]
    even = i % 2 == 0
    w_in = w_in_even if even else w_in_odd
    w_out = w_out_even if even else w_out_odd
    lw = i // 2
    rope_q = (N_HEADS_MOBA * HEAD_DIM // tn) if even else mw // tn
    common = dict(layer=lw, tm=tm, tn=tn, half=HEAD_DIM // 2)
    qs = HEAD_DIM ** -0.5
    wa = N_HEADS_MOBA * HEAD_DIM
    colscale = jnp.where((jnp.arange(mw) < wa) | (not even), qs * LOG2E, qs).astype(F32)[None, :]
    (q16,) = norm_proj(x, g, w_in, cos, sin, col0=0, ncols=mw, rope_tiles=rope_q, want_bf16=True,
                       colscale=colscale, **common)
    k_out = norm_proj(x, g, w_in, cos, sin, col0=mw, ncols=mw, rope_tiles=rope_q, want_f32=True,
                      want_bf16=True, want_kmean=even, **common)
    v32, v16 = norm_proj(x, g, w_in, cos, sin, col0=2 * mw, ncols=mw, rope_tiles=0, want_f32=True,
                         want_bf16=True, **common)
    k32, k16 = k_out[0], k_out[1]
    sh = (b, l, mw)
    if even:
        kmean = k_out[2].reshape(b, l // MOBA_BLOCK, mw)
        o = even_attention(q16.reshape(sh), k16.reshape(sh), v16.reshape(sh), kmean)
        kidx = None
    else:
        c1 = IDX_HEADS * IDX_DIM
        tail, kidx = norm_proj(x, g, w_tail, cos_t, sin_t, layer=lw, col0=0, ncols=TAIL_WIDTH, tm=tm,
                               tn=TAIL_WIDTH, rope_tiles=1, half=IDX_DIM // 2, want_f32=True,
                               side_cols=(c1, IDX_DIM))
        qidx = tail[:, :c1].reshape(b, l, IDX_HEADS, IDX_DIM).transpose(0, 2, 1, 3).astype(BF16)
        kidx = kidx.reshape(b, l, IDX_DIM)
        widx = tail[:, c1 + IDX_DIM:TAIL_COLS].reshape(b, l, IDX_HEADS).transpose(0, 2, 1)
        o = dsa_attention(q16.reshape(sh), k16.reshape(sh), v16.reshape(sh), qidx, kidx.astype(BF16), widx)
    x = matmul_res(o.reshape(b * l, mw), w_out, x, layer=lw, tm=tm, tn=tn)
    hmid = gate_up(x, ffn_norm[i][None, :], w_gate, w_up, layer=i, tm=tm, tn=512)
    x = matmul_res(hmid, w_down, x, layer=i, tm=tm, tn=512)
    return x, k32, v32, kidx


def _decode_layer(x, i, page_table, ck, cv, cache_kidx, wts, tabs):
    (attn_norm, ffn_norm, w_in_even, w_out_even, w_in_odd, w_out_odd, w_tail, w_gate, w_up, w_down) = wts
    cos, sin, cos_t, sin_t = tabs
    s = x.shape[0]
    mw = N_HEADS * HEAD_DIM
    tm, tn = s, 1024
    g = attn_norm[i][None, :]
    even = i % 2 == 0
    w_in = w_in_even if even else w_in_odd
    w_out = w_out_even if even else w_out_odd
    lw = i // 2
    rope_q = (N_HEADS_MOBA * HEAD_DIM // tn) if even else mw // tn
    common = dict(layer=lw, tm=tm, tn=tn, half=HEAD_DIM // 2, want_f32=True)
    (q,) = norm_proj(x, g, w_in, cos, sin, col0=0, ncols=mw, rope_tiles=rope_q, **common)
    (k,) = norm_proj(x, g, w_in, cos, sin, col0=mw, ncols=mw, rope_tiles=rope_q, **common)
    (v,) = norm_proj(x, g, w_in, cos, sin, col0=2 * mw, ncols=mw, rope_tiles=0, **common)
    hsh = (s, N_HEADS, HEAD_DIM)
    q3, k3, v3 = q.reshape(hsh), k.reshape(hsh), v.reshape(hsh)
    if even:
        page_rows = ck.shape[:2] + (PAGE_SIZE * N_HEADS, HEAD_DIM)
        o_sb = sb_decode_attention(page_table, q3, ck.reshape(page_rows), cv.reshape(page_rows), i)
        ids = moba_decode_gate(page_table, q3, ck, i)
        ids = jnp.transpose(ids[:, :MOBA_TOPK, :N_HEADS_MOBA], (0, 2, 1))
        o_moba = moba_decode_attention(page_table, ids, q3, k3, v3, ck, cv, i)
        o = jnp.concatenate([o_moba, o_sb], axis=1)
        kidx = None
    else:
        c1 = IDX_HEADS * IDX_DIM
        tail, kidx = norm_proj(x, g, w_tail, cos_t, sin_t, layer=lw, col0=0, ncols=TAIL_WIDTH, tm=tm,
                               tn=TAIL_WIDTH, rope_tiles=1, half=IDX_DIM // 2, want_f32=True,
                               side_cols=(c1, IDX_DIM))
        qidx = tail[:, :c1].reshape(s, IDX_HEADS, IDX_DIM)
        kidx = kidx.reshape(s, 1, IDX_DIM)
        widx = tail[:, c1 + IDX_DIM:TAIL_COLS].reshape(s, IDX_HEADS, 1)
        idx = dsa_decode_select(page_table, qidx, widx, kidx, cache_kidx, i // 2)
        o = dsa_decode_attention(page_table, idx[:, :, 0], q3, k3, v3, ck, cv, i)
    x = matmul_res(o.reshape(s, mw).astype(BF16), w_out, x, layer=lw, tm=tm, tn=tn)
    hmid = gate_up(x, ffn_norm[i][None, :], w_gate, w_up, layer=i, tm=tm, tn=512)
    x = matmul_res(hmid, w_down, x, layer=i, tm=tm, tn=512)
    return x, k, v, kidx


def kernel(x_prompt, x_sample, cache_k, cache_v, cache_kidx, page_table, attn_norm, ffn_norm, final_norm,
           w_in_even, w_out_even, w_in_odd, w_out_odd, w_gate, w_up, w_down):
    b, l, d = x_prompt.shape
    depth = attn_norm.shape[0]
    mw = N_HEADS * HEAD_DIM
    w_tail = jnp.pad(w_in_odd[:, :, 3 * mw:], ((0, 0), (0, 0), (0, TAIL_WIDTH - TAIL_COLS))).astype(BF16)
    wts = (attn_norm, ffn_norm, w_in_even.astype(BF16), w_out_even.astype(BF16),
           w_in_odd.astype(BF16), w_out_odd.astype(BF16), w_tail,
           w_gate.astype(BF16), w_up.astype(BF16), w_down.astype(BF16))
    pos = np.arange(l, dtype=np.int32)
    tabs = tuple(jnp.asarray(a) for a in _rope_tables(pos, HEAD_DIM // 2, 1) + _idx_tables(pos))

    x = x_prompt.reshape(b * l, d)
    ks, vs, kis = [], [], []
    for i in range(depth):
        x, k32, v32, kidx = _prompt_layer(x, i, pos, b, l, wts, tabs)
        ks.append(k32.reshape(b, l, N_HEADS, HEAD_DIM))
        vs.append(v32.reshape(b, l, N_HEADS, HEAD_DIM))
        if kidx is not None:
            kis.append(kidx)
    y_prompt = rmsnorm(x, final_norm[None, :], tm=512).reshape(b, l, d)

    db = x_sample.shape[0]
    assert x_sample.shape[1] == 1
    past_len = page_table.shape[1] * PAGE_SIZE
    ck, cv = cache_k, cache_v
    pos_d = np.full((db,), past_len, np.int32)
    tabs_d = tuple(jnp.asarray(a) for a in _rope_tables(pos_d, HEAD_DIM // 2, 1) + _idx_tables(pos_d))
    xd = x_sample.reshape(db, d)
    kd, vd, kid = [], [], []
    for i in range(depth):
        xd, k32, v32, kidx = _decode_layer(xd, i, page_table, ck, cv, cache_kidx, wts, tabs_d)
        kd.append(k32.reshape(db, 1, N_HEADS, HEAD_DIM))
        vd.append(v32.reshape(db, 1, N_HEADS, HEAD_DIM))
        if kidx is not None:
            kid.append(kidx)
    y_sample = rmsnorm(xd, final_norm[None, :], tm=db).reshape(db, 1, d)
    return (y_prompt, y_sample, jnp.stack(ks), jnp.stack(vs), jnp.stack(kis),
            jnp.stack(kd), jnp.stack(vd), jnp.stack(kid))
```
